```python
import jax, jax.numpy as jnp
from jax import lax
import numpy as np

D_MODEL = 1024
BATCH = 16
SEQ = 2048
DEPTH = 1
DEC_BATCH = 8
DEC_SEQ = 16
PAST_LEN = 2048

CHUNK = 64
D_A = 1024
D_B = 1024
GMLP_CHUNK = 128
A_GROUPS = 8
A_GROUP_DIM = D_A // A_GROUPS
CONV_WIDTH = 31
CONV_HIST = CONV_WIDTH - 1
EPS = 1e-6
SPLITS = (D_A, 2 * D_A, 3 * D_A, 3 * D_A + D_B, 3 * D_A + 2 * D_B, 3 * D_A + 3 * D_B, 3 * D_A + 3 * D_B + D_MODEL)
IN_COLS = 3 * D_A + 3 * D_B + 2 * D_MODEL

kernel_name = 'streaming_gmlp_conformer_hybrid'


def _rmsnorm(x, g):
    xf = x.astype(jnp.float32)
    y = xf * lax.rsqrt(jnp.mean(xf * xf, axis=-1, keepdims=True) + EPS)
    return (y * g.astype(jnp.float32)).astype(x.dtype)


def _layernorm(x, g, b):
    xf = x.astype(jnp.float32)
    mu = jnp.mean(xf, axis=-1, keepdims=True)
    xc = xf - mu
    var = jnp.mean(xc * xc, axis=-1, keepdims=True)
    y = xc * lax.rsqrt(var + EPS) * g.astype(jnp.float32) + b.astype(jnp.float32)
    return y.astype(x.dtype)


def _spatial_proj(v, w_spatial, b_spatial):
    bsz, t, _ = v.shape
    pad = (-t) % GMLP_CHUNK
    vp = jnp.pad(v, ((0, 0), (0, pad), (0, 0)))
    n = (t + pad) // GMLP_CHUNK
    vp = vp.reshape(bsz, n, GMLP_CHUNK, A_GROUPS, A_GROUP_DIM)
    mask = jnp.tril(jnp.ones((GMLP_CHUNK, GMLP_CHUNK), dtype=bool))
    wm = jnp.where(mask[None], w_spatial, jnp.zeros_like(w_spatial))
    s = jnp.einsum('hqk,bnkhc->bnqhc', wm, vp) + b_spatial.T[None, None, :, :, None]
    return s.reshape(bsz, n * GMLP_CHUNK, D_A)[:, :t]


def _layer(x, c, conv_hist, w_ada, b_ada, norm_g, w_in, b_in, ln_v_g, ln_v_b,
           w_spatial, b_spatial, conv_w, conv_b, ln_c_g, ln_c_b, w_o_a, w_o_b, w_out):
    mod = jax.nn.silu(c) @ w_ada + b_ada
    shift, scale, gate = jnp.split(mod[:, None, :], 3, axis=-1)
    h = _rmsnorm(x, norm_g) * (1 + scale) + shift
    z = h @ w_in + b_in
    u, v, gate_a, glu_a, glu_b, gate_b, merge_a, merge_b = jnp.split(z, SPLITS, axis=-1)
    v_n = _layernorm(jax.nn.gelu(v), ln_v_g, ln_v_b)
    y_a = jax.nn.gelu(u) * _spatial_proj(v_n, w_spatial, b_spatial)
    y_a = (y_a * jax.nn.silu(gate_a)) @ w_o_a
    g = glu_a * jax.nn.sigmoid(glu_b)
    g_cat = jnp.concatenate([conv_hist.astype(g.dtype), g], axis=1)
    dw = lax.conv_general_dilated(g_cat, conv_w[:, None, :].astype(g.dtype), (1,), 'VALID',
                                  dimension_numbers=('NWC', 'WIO', 'NWC'),
                                  feature_group_count=D_B) + conv_b
    y_b = jax.nn.silu(_layernorm(dw, ln_c_g, ln_c_b))
    y_b = (y_b * jax.nn.silu(gate_b)) @ w_o_b
    m = jax.nn.sigmoid(merge_a) * y_a + jax.nn.sigmoid(merge_b) * y_b
    x = x + gate * (m @ w_out)
    return x, g_cat[:, -CONV_HIST:], v_n


def setup_inputs(seed: int = 0) -> dict:
    key = jax.random.key(seed)
    ks = jax.random.split(key, 24)
    f32 = jnp.float32
    nrm = lambda k, shape, s: jax.random.normal(k, shape, f32) * s
    return {
        'x_prompt': nrm(ks[0], (BATCH, SEQ, D_MODEL), 1.0),
        'x_sample': nrm(ks[1], (DEC_BATCH, DEC_SEQ, D_MODEL), 1.0),
        'state_conv': nrm(ks[2], (DEPTH, DEC_BATCH, CONV_HIST, D_B), 0.5),
        'c_prompt': nrm(ks[3], (BATCH, D_MODEL), 1.0),
        'c_sample': nrm(ks[4], (DEC_BATCH, D_MODEL), 1.0),
        'w_ada': nrm(ks[5], (DEPTH, D_MODEL, 3 * D_MODEL), D_MODEL ** -0.5),
        'b_ada': nrm(ks[6], (DEPTH, 3 * D_MODEL), 0.02),
        'norm_g': 1.0 + nrm(ks[7], (DEPTH, D_MODEL), 0.02),
        'w_in': nrm(ks[8], (DEPTH, D_MODEL, IN_COLS), D_MODEL ** -0.5),
        'b_in': nrm(ks[9], (DEPTH, IN_COLS), 0.02),
        'ln_v_g': 1.0 + nrm(ks[10], (DEPTH, D_A), 0.02),
        'ln_v_b': nrm(ks[11], (DEPTH, D_A), 0.02),
        'w_spatial': nrm(ks[12], (DEPTH, A_GROUPS, GMLP_CHUNK, GMLP_CHUNK), GMLP_CHUNK ** -0.5),
        'b_spatial': 1.0 + nrm(ks[13], (DEPTH, A_GROUPS, GMLP_CHUNK), 0.02),
        'conv_w': nrm(ks[14], (DEPTH, CONV_WIDTH, D_B), CONV_WIDTH ** -0.5),
        'conv_b': nrm(ks[15], (DEPTH, D_B), 0.02),
        'ln_c_g': 1.0 + nrm(ks[16], (DEPTH, D_B), 0.02),
        'ln_c_b': nrm(ks[17], (DEPTH, D_B), 0.02),
        'w_o_a': nrm(ks[18], (DEPTH, D_A, D_MODEL), D_A ** -0.5),
        'w_o_b': nrm(ks[19], (DEPTH, D_B, D_MODEL), D_B ** -0.5),
        'w_out': nrm(ks[20], (DEPTH, D_MODEL, D_MODEL), D_MODEL ** -0.5),
        'final_g': 1.0 + nrm(ks[21], (D_MODEL,), 0.02),
    }


def reference(x_prompt, x_sample, state_conv, c_prompt, c_sample, w_ada, b_ada, norm_g,
              w_in, b_in, ln_v_g, ln_v_b, w_spatial, b_spatial, conv_w, conv_b,
              ln_c_g, ln_c_b, w_o_a, w_o_b, w_out, final_g):
    hp, hs = x_prompt, x_sample
    conv_p_out, conv_s_out, v_s_out = [], [], []
    for l in range(DEPTH):
        p = (w_ada[l], b_ada[l], norm_g[l], w_in[l], b_in[l], ln_v_g[l], ln_v_b[l],
             w_spatial[l], b_spatial[l], conv_w[l], conv_b[l], ln_c_g[l], ln_c_b[l],
             w_o_a[l], w_o_b[l], w_out[l])
        zero_hist = jnp.zeros((hp.shape[0], CONV_HIST, D_B), hp.dtype)
        hp, conv_p, _ = _layer(hp, c_prompt, zero_hist, *p)
        hs, conv_s, v_s = _layer(hs, c_sample, state_conv[l], *p)
        conv_p_out.append(conv_p)
        conv_s_out.append(conv_s)
        v_s_out.append(v_s)
    y_prompt = _rmsnorm(hp, final_g)
    y_sample = _rmsnorm(hs, final_g)
    new_conv_prompt = jnp.stack(conv_p_out)
    new_conv_sample = jnp.stack(conv_s_out)
    new_gmlp_v_sample = jnp.stack(v_s_out)
    return (y_prompt, y_sample, new_conv_prompt, new_conv_sample, new_gmlp_v_sample)
```

```python
import functools

import jax
import jax.numpy as jnp
from jax import lax
from jax.experimental import pallas as pl
from jax.experimental.pallas import tpu as pltpu

D = 1024
N_IN_BLOCKS = 8
GMLP_CHUNK = 128
N_HEADS = 8
HEAD_DIM = D // N_HEADS
CONV_WIDTH = 31
CONV_HIST = CONV_WIDTH - 1
EPS = 1e-6

LANES = 128
SUBLANES = 8
COL_BLOCKS = D // LANES
HIST_PAD = 32
HIST_SKIP = HIST_PAD - CONV_HIST
ROW_CHUNK = 16
CONV_TBLOCK = 8
PROMPT_TILE = 256
VMEM_LIMIT_BYTES = 56 * 1024 * 1024

_GELU_C = 0.7978845608028654
_GELU_K = 0.044715


def _gelu(x):
    return 0.5 * x * (1.0 + jnp.tanh(_GELU_C * (x + _GELU_K * (x * x * x))))


def _sigmoid(x):
    return 0.5 * jnp.tanh(0.5 * x) + 0.5


def _silu(x):
    return x * _sigmoid(x)


def _layernorm(x, g, b):
    mu = jnp.mean(x, axis=-1, keepdims=True)
    xc = x - mu
    var = jnp.mean(xc * xc, axis=-1, keepdims=True)
    return xc * lax.rsqrt(var + EPS) * g + b


def _rms_scale(x):
    return lax.rsqrt(jnp.mean(x * x, axis=-1, keepdims=True) + EPS)


def _adaln_body(c_ref, w_ref, b_ref, o_ref):
    c = c_ref[...]
    a = _silu(c).astype(jnp.bfloat16)
    o_ref[...] = jnp.dot(a, w_ref[...].astype(jnp.bfloat16),
                         preferred_element_type=jnp.float32) + b_ref[...]


def _adaln(c_all, w_ada, b_ada):
    n = c_all.shape[0]
    blk = 3 * D // 4
    return pl.pallas_call(
        _adaln_body,
        grid=(4,),
        in_specs=[pl.BlockSpec((n, D), lambda j: (0, 0)),
                  pl.BlockSpec((D, blk), lambda j: (0, j)),
                  pl.BlockSpec((1, blk), lambda j: (0, j))],
        out_specs=pl.BlockSpec((n, blk), lambda j: (0, j)),
        out_shape=jax.ShapeDtypeStruct((n, 3 * D), jnp.float32),
        name="adaln",
    )(c_all, w_ada, b_ada)


def _layer_tile(tm, with_state, x_ref, mod_ref, state_ref, norm_g_ref, w_in_ref, b_in_ref,
                ln_v_g_ref, ln_v_b_ref, wm_ref, bsp_ref, conv_w_ref, conv_b_ref,
                ln_c_g_ref, ln_c_b_ref, w_o_a_ref, w_o_b_ref, w_out_ref, final_g_ref,
                y_ref, hist_ref, vn_out_ref,
                hb_ref, za_ref, zb_ref, zm_ref, vn_ref, sp_ref, act_ref, ya_ref, yb_ref,
                s2_ref, d2_ref, rhs_ref):
    t = pl.program_id(1)
    f32 = jnp.float32
    bf16 = jnp.bfloat16
    hist_rows = HIST_PAD * COL_BLOCKS

    @pl.when(t == 0)
    def _():
        if with_state:
            s2_ref[pl.ds(0, HIST_SKIP * COL_BLOCKS), :] = jnp.zeros((HIST_SKIP * COL_BLOCKS, LANES), f32)
            s2_ref[pl.ds(HIST_SKIP * COL_BLOCKS, CONV_HIST * COL_BLOCKS), :] = state_ref[0]
        else:
            s2_ref[pl.ds(0, hist_rows), :] = jnp.zeros((hist_rows, LANES), f32)

    shift = mod_ref[0, :, 0:D]
    scale1 = 1.0 + mod_ref[0, :, D:2 * D]
    gate = mod_ref[0, :, 2 * D:3 * D]
    gs = norm_g_ref[...] * scale1

    for r in range(0, tm, ROW_CHUNK):
        x = x_ref[0, pl.ds(r, ROW_CHUNK), :]
        h = (x * _rms_scale(x)) * gs + shift
        hb_ref[pl.ds(r, ROW_CHUNK), :] = h.astype(bf16)

    def in_proj(dst_ref, c0, nblk):
        dst_ref[:, 0:nblk * D] = jnp.dot(
            hb_ref[...], w_in_ref[:, c0 * D:(c0 + nblk) * D],
            preferred_element_type=f32) + b_in_ref[:, c0 * D:(c0 + nblk) * D]

    in_proj(za_ref, 0, 3)
    ln_v_g = ln_v_g_ref[...]
    ln_v_b = ln_v_b_ref[...]
    for r in range(0, tm, ROW_CHUNK):
        v = za_ref[pl.ds(r, ROW_CHUNK), D:2 * D]
        vn = _layernorm(_gelu(v), ln_v_g, ln_v_b)
        if vn_out_ref is not None:
            vn_out_ref[0, pl.ds(r, ROW_CHUNK), :] = vn
        vn_ref[pl.ds(r, ROW_CHUNK), :] = vn.astype(bf16)

    if tm % GMLP_CHUNK == 0:
        n_chunks = tm // GMLP_CHUNK
        for hd in range(N_HEADS):
            cols = slice(hd * HEAD_DIM, (hd + 1) * HEAD_DIM)
            rhs = jnp.concatenate(
                [vn_ref[n * GMLP_CHUNK:(n + 1) * GMLP_CHUNK, cols] for n in range(n_chunks)], axis=1)
            s = jnp.dot(wm_ref[hd], rhs, preferred_element_type=f32)
            for n in range(n_chunks):
                sp_ref[n * GMLP_CHUNK:(n + 1) * GMLP_CHUNK, cols] = (
                    s[:, n * HEAD_DIM:(n + 1) * HEAD_DIM] + bsp_ref[:, cols])
    else:
        rhs_ref[...] = jnp.zeros(rhs_ref.shape, bf16)
        rhs_ref[0:tm, :] = vn_ref[...]
        for hd in range(N_HEADS):
            cols = slice(hd * HEAD_DIM, (hd + 1) * HEAD_DIM)
            s = jnp.dot(wm_ref[hd, 0:tm, :], rhs_ref[:, cols], preferred_element_type=f32)
            sp_ref[:, cols] = s + bsp_ref[0:tm, cols]

    for r in range(0, tm, ROW_CHUNK):
        u = za_ref[pl.ds(r, ROW_CHUNK), 0:D]
        ga = za_ref[pl.ds(r, ROW_CHUNK), 2 * D:3 * D]
        ya = _gelu(u) * sp_ref[pl.ds(r, ROW_CHUNK), :] * _silu(ga)
        act_ref[pl.ds(r, ROW_CHUNK), :] = ya.astype(bf16)
    ya_ref[...] = jnp.dot(act_ref[...], w_o_a_ref[...], preferred_element_type=f32)

    in_proj(zb_ref, 3, 3)
    for c in range(COL_BLOCKS):
        cols = slice(c * LANES, (c + 1) * LANES)
        g = zb_ref[:, cols] * _sigmoid(zb_ref[:, D + c * LANES:D + (c + 1) * LANES])
        s2_ref[pl.ds(hist_rows + c, tm, stride=COL_BLOCKS), :] = g

    tb = min(CONV_TBLOCK, tm)
    wt = [jnp.tile(conv_w_ref[k * COL_BLOCKS:(k + 1) * COL_BLOCKS, :], (tb, 1)) for k in range(CONV_WIDTH)]
    for t0 in range(0, tm, tb):
        acc = s2_ref[pl.ds((t0 + HIST_SKIP) * COL_BLOCKS, tb * COL_BLOCKS), :] * wt[0]
        for k in range(1, CONV_WIDTH):
            acc = acc + s2_ref[pl.ds((t0 + k + HIST_SKIP) * COL_BLOCKS, tb * COL_BLOCKS), :] * wt[k]
        d2_ref[pl.ds(t0 * COL_BLOCKS, tb * COL_BLOCKS), :] = acc

    hist_ref[0] = s2_ref[pl.ds((tm + HIST_SKIP) * COL_BLOCKS, CONV_HIST * COL_BLOCKS), :]
    s2_ref[pl.ds(0, hist_rows), :] = s2_ref[pl.ds(tm * COL_BLOCKS, hist_rows), :]

    conv_b = conv_b_ref[...]
    ln_c_g = ln_c_g_ref[...]
    ln_c_b = ln_c_b_ref[...]
    for r in range(0, tm, ROW_CHUNK):
        dw = jnp.concatenate(
            [d2_ref[pl.ds(r * COL_BLOCKS + c, ROW_CHUNK, stride=COL_BLOCKS), :] for c in range(COL_BLOCKS)],
            axis=1) + conv_b
        gb = zb_ref[pl.ds(r, ROW_CHUNK), 2 * D:3 * D]
        yb = _silu(_layernorm(dw, ln_c_g, ln_c_b)) * _silu(gb)
        act_ref[pl.ds(r, ROW_CHUNK), :] = yb.astype(bf16)
    yb_ref[...] = jnp.dot(act_ref[...], w_o_b_ref[...], preferred_element_type=f32)

    in_proj(zm_ref, 6, 2)
    for r in range(0, tm, ROW_CHUNK):
        rows = pl.ds(r, ROW_CHUNK)
        m = (_sigmoid(zm_ref[rows, 0:D]) * ya_ref[rows, :]
             + _sigmoid(zm_ref[rows, D:2 * D]) * yb_ref[rows, :])
        act_ref[rows, :] = m.astype(bf16)
    za_ref[:, 0:D] = jnp.dot(act_ref[...], w_out_ref[...], preferred_element_type=f32)
    final_g = final_g_ref[...]
    for r in range(0, tm, ROW_CHUNK):
        rows = pl.ds(r, ROW_CHUNK)
        xo = x_ref[0, rows, :] + gate * za_ref[rows, 0:D]
        y_ref[0, rows, :] = (xo * _rms_scale(xo)) * final_g


def _prompt_body(tm, *refs):
    (x_ref, mod_ref, norm_g_ref, w_in_ref, b_in_ref, ln_v_g_ref, ln_v_b_ref, wm_ref, bsp_ref,
     conv_w_ref, conv_b_ref, ln_c_g_ref, ln_c_b_ref, w_o_a_ref, w_o_b_ref, w_out_ref, final_g_ref,
     y_ref, hist_ref, *scratch) = refs
    _layer_tile(tm, False, x_ref, mod_ref, None, norm_g_ref, w_in_ref, b_in_ref, ln_v_g_ref, ln_v_b_ref,
                wm_ref, bsp_ref, conv_w_ref, conv_b_ref, ln_c_g_ref, ln_c_b_ref,
                w_o_a_ref, w_o_b_ref, w_out_ref, final_g_ref, y_ref, hist_ref, None, *scratch, None)


def _sample_body(tm, *refs):
    (x_ref, mod_ref, state_ref, norm_g_ref, w_in_ref, b_in_ref, ln_v_g_ref, ln_v_b_ref, wm_ref, bsp_ref,
     conv_w_ref, conv_b_ref, ln_c_g_ref, ln_c_b_ref, w_o_a_ref, w_o_b_ref, w_out_ref, final_g_ref,
     y_ref, hist_ref, vn_out_ref, *scratch) = refs
    _layer_tile(tm, True, x_ref, mod_ref, state_ref, norm_g_ref, w_in_ref, b_in_ref, ln_v_g_ref, ln_v_b_ref,
                wm_ref, bsp_ref, conv_w_ref, conv_b_ref, ln_c_g_ref, ln_c_b_ref,
                w_o_a_ref, w_o_b_ref, w_out_ref, final_g_ref, y_ref, hist_ref, vn_out_ref, *scratch)


def _resident(shape):
    zeros = (0,) * len(shape)
    return pl.BlockSpec(shape, lambda b, t: zeros, pipeline_mode=pl.Buffered(1))


def _tile_scratch(tm):
    f32, bf16 = jnp.float32, jnp.bfloat16
    return [
        pltpu.VMEM((tm, D), bf16),
        pltpu.VMEM((tm, 3 * D), f32),
        pltpu.VMEM((tm, 3 * D), f32),
        pltpu.VMEM((tm, 2 * D), f32),
        pltpu.VMEM((tm, D), bf16),
        pltpu.VMEM((tm, D), f32),
        pltpu.VMEM((tm, D), bf16),
        pltpu.VMEM((tm, D), f32),
        pltpu.VMEM((tm, D), f32),
        pltpu.VMEM(((tm + HIST_PAD) * COL_BLOCKS, LANES), f32),
        pltpu.VMEM((tm * COL_BLOCKS, LANES), f32),
    ]


def _layer_call(x, mod, state2d, params, tm, name):
    bsz, tlen, _ = x.shape
    assert tlen % tm == 0 and tm % ROW_CHUNK == 0
    nt = tlen // tm
    with_state = state2d is not None
    (norm_g, w_in, b_in, ln_v_g, ln_v_b, wm, bsp, conv_w2, conv_b, ln_c_g, ln_c_b,
     w_o_a, w_o_b, w_out, final_g) = params

    in_specs = [pl.BlockSpec((1, tm, D), lambda b, t: (b, t, 0)),
                pl.BlockSpec((1, 1, 3 * D), lambda b, t: (b, 0, 0))]
    args = [x, mod]
    if with_state:
        in_specs.append(pl.BlockSpec((1, CONV_HIST * COL_BLOCKS, LANES), lambda b, t: (b, 0, 0)))
        args.append(state2d)
    for p in params:
        in_specs.append(_resident(p.shape))
        args.append(p)

    out_shape = [jax.ShapeDtypeStruct((bsz, tlen, D), jnp.float32),
                 jax.ShapeDtypeStruct((bsz, CONV_HIST * COL_BLOCKS, LANES), jnp.float32)]
    out_specs = [pl.BlockSpec((1, tm, D), lambda b, t: (b, t, 0)),
                 pl.BlockSpec((1, CONV_HIST * COL_BLOCKS, LANES), lambda b, t: (b, 0, 0))]
    scratch = _tile_scratch(tm)
    if with_state:
        out_shape.append(jax.ShapeDtypeStruct((bsz, tlen, D), jnp.float32))
        out_specs.append(pl.BlockSpec((1, tm, D), lambda b, t: (b, t, 0)))
        scratch.append(pltpu.VMEM((GMLP_CHUNK, D), jnp.bfloat16))
        body = functools.partial(_sample_body, tm)
    else:
        body = functools.partial(_prompt_body, tm)

    return pl.pallas_call(
        body,
        grid=(bsz, nt),
        in_specs=in_specs,
        out_specs=out_specs,
        out_shape=out_shape,
        scratch_shapes=scratch,
        compiler_params=pltpu.CompilerParams(
            dimension_semantics=("arbitrary", "arbitrary"),
            vmem_limit_bytes=VMEM_LIMIT_BYTES),
        name=name,
    )(*args)


def kernel(x_prompt, x_sample, state_conv, c_prompt, c_sample, w_ada, b_ada, norm_g, w_in, b_in, ln_v_g, ln_v_b, w_spatial, b_spatial, conv_w, conv_b, ln_c_g, ln_c_b, w_o_a, w_o_b, w_out, final_g):
    assert w_ada.shape[0] == 1, "single-layer kernel"
    bf16 = jnp.bfloat16
    n_prompt = x_prompt.shape[0]
    n_sample = x_sample.shape[0]

    mod = _adaln(jnp.concatenate([c_prompt, c_sample], axis=0), w_ada[0], b_ada[0][None])
    mod = mod[:, None, :]

    tril = jnp.tril(jnp.ones((GMLP_CHUNK, GMLP_CHUNK), dtype=bool))
    wm = jnp.where(tril[None], w_spatial[0], 0.0).astype(bf16)
    bsp = jnp.repeat(b_spatial[0].T, HEAD_DIM, axis=1)
    row = lambda a: a.reshape(1, -1)
    params = (row(norm_g[0]), w_in[0].astype(bf16), row(b_in[0]), row(ln_v_g[0]), row(ln_v_b[0]), wm, bsp,
              conv_w[0].reshape(CONV_WIDTH * COL_BLOCKS, LANES), row(conv_b[0]), row(ln_c_g[0]), row(ln_c_b[0]),
              w_o_a[0].astype(bf16), w_o_b[0].astype(bf16), w_out[0].astype(bf16), row(final_g))

    y_prompt, hist_prompt = _layer_call(x_prompt, mod[:n_prompt], None, params, PROMPT_TILE, "prompt_layer")
    state2d = state_conv[0].reshape(n_sample, CONV_HIST * COL_BLOCKS, LANES)
    y_sample, hist_sample, vn_sample = _layer_call(
        x_sample, mod[n_prompt:], state2d, params, x_sample.shape[1], "sample_layer")

    new_conv_prompt = hist_prompt.reshape(1, n_prompt, CONV_HIST, D)
    new_conv_sample = hist_sample.reshape(1, n_sample, CONV_HIST, D)
    return (y_prompt, y_sample, new_conv_prompt, new_conv_sample, vn_sample[None])
```

```python
import functools

import jax
import jax.numpy as jnp
from jax import lax
from jax.experimental import pallas as pl
from jax.experimental.pallas import tpu as pltpu

D = 1024
N_IN_BLOCKS = 8
GMLP_CHUNK = 128
N_HEADS = 8
HEAD_DIM = D // N_HEADS
CONV_WIDTH = 31
CONV_HIST = CONV_WIDTH - 1
EPS = 1e-6

LANES = 128
COL_BLOCKS = D // LANES
HIST_PAD = 32
HIST_SKIP = HIST_PAD - CONV_HIST
HIST_ROWS = HIST_PAD * COL_BLOCKS
ROW_CHUNK = 16
CONV_TBLOCK = 4
PROMPT_TILE = 256
VMEM_LIMIT_BYTES = 56 * 1024 * 1024

Z_U, Z_V, Z_GATE_A, Z_GLU_A, Z_GLU_B, Z_GATE_B, Z_MERGE_A, Z_MERGE_B = range(N_IN_BLOCKS)

_GELU_C = 0.7978845608028654
_GELU_K = 0.044715


def _gelu(x):
    return 0.5 * x * (1.0 + jnp.tanh(_GELU_C * (x + _GELU_K * (x * x * x))))


def _sigmoid(x):
    return 0.5 * jnp.tanh(0.5 * x) + 0.5


def _silu(x):
    return x * _sigmoid(x)


def _layernorm(x, g, b):
    mu = jnp.mean(x, axis=-1, keepdims=True)
    xc = x - mu
    var = jnp.mean(xc * xc, axis=-1, keepdims=True)
    return xc * lax.rsqrt(var + EPS) * g + b


def _rms_scale(x):
    return lax.rsqrt(jnp.mean(x * x, axis=-1, keepdims=True) + EPS)


def _zero_of(token):
    return (pltpu.bitcast(token, jnp.uint32) >> 16) >> 16


def _adaln_body(c_ref, w_ref, b_ref, o_ref):
    a = _silu(c_ref[...]).astype(jnp.bfloat16)
    o_ref[...] = jnp.dot(a, w_ref[...].astype(jnp.bfloat16),
                         preferred_element_type=jnp.float32) + b_ref[...]


def _adaln(c_all, w_ada, b_ada):
    n = c_all.shape[0]
    blk = 3 * D // 4
    return pl.pallas_call(
        _adaln_body,
        grid=(4,),
        in_specs=[pl.BlockSpec((n, D), lambda j: (0, 0)),
                  pl.BlockSpec((D, blk), lambda j: (0, j)),
                  pl.BlockSpec((1, blk), lambda j: (0, j))],
        out_specs=pl.BlockSpec((n, blk), lambda j: (0, j)),
        out_shape=jax.ShapeDtypeStruct((n, 3 * D), jnp.float32),
        name="adaln",
    )(c_all, w_ada, b_ada)


class _Weights:
    def __init__(self, refs):
        (self.norm_g, self.w_in, self.b_in, self.ln_v_g, self.ln_v_b, self.wm, self.bsp, self.conv_w,
         self.conv_b, self.ln_c_g, self.ln_c_b, self.w_o_a, self.w_o_b, self.w_out, self.final_g) = refs


N_WEIGHTS = 15


class _Scratch:
    def __init__(self, refs):
        (self.hb, self.zglu, self.za, self.zgm, self.vn, self.sp, self.acta, self.actb, self.actm,
         self.ya, self.yb, self.o, self.s2, self.d2) = refs


def _tile_scratch(tm):
    f32, bf16 = jnp.float32, jnp.bfloat16
    return [
        pltpu.VMEM((tm, D), bf16),
        pltpu.VMEM((tm, 2 * D), f32),
        pltpu.VMEM((tm, 3 * D), f32),
        pltpu.VMEM((tm, 3 * D), f32),
        pltpu.VMEM((tm, D), bf16),
        pltpu.VMEM((tm, D), f32),
        pltpu.VMEM((tm, D), bf16),
        pltpu.VMEM((tm, D), bf16),
        pltpu.VMEM((tm, D), bf16),
        pltpu.VMEM((tm, D), f32),
        pltpu.VMEM((tm, D), f32),
        pltpu.VMEM((tm, D), f32),
        pltpu.VMEM(((tm + HIST_PAD) * COL_BLOCKS, LANES), f32),
        pltpu.VMEM((tm * COL_BLOCKS, LANES), f32),
    ]


def _in_proj(w, sc, dst_ref, blk0, nblk):
    cols = slice(blk0 * D, (blk0 + nblk) * D)
    dst_ref[...] = jnp.dot(sc.hb[...], w.w_in[:, cols],
                           preferred_element_type=jnp.float32) + w.b_in[:, cols]


def _tile_head(tm, x_ref, mod_ref, w, sc):
    shift = mod_ref[0, :, 0:D]
    gs = w.norm_g[...] * (1.0 + mod_ref[0, :, D:2 * D])
    for r in range(0, tm, ROW_CHUNK):
        x = x_ref[0, pl.ds(r, ROW_CHUNK), :]
        h = (x * _rms_scale(x)) * gs + shift
        sc.hb[pl.ds(r, ROW_CHUNK), :] = h.astype(jnp.bfloat16)
    _in_proj(w, sc, sc.zglu, Z_GLU_A, 2)


def _tile_body(tm, x_ref, mod_ref, next_head, w, sc, y_ref, hist_ref, vn_out_ref, rhs_ref):
    f32, bf16 = jnp.float32, jnp.bfloat16

    for c in range(COL_BLOCKS):
        a = sc.zglu[:, c * LANES:(c + 1) * LANES]
        b = sc.zglu[:, D + c * LANES:D + (c + 1) * LANES]
        sc.s2[pl.ds(HIST_ROWS + c, tm, stride=COL_BLOCKS), :] = a * _sigmoid(b)

    _in_proj(w, sc, sc.za, Z_U, 3)

    tb = min(CONV_TBLOCK, tm)
    token = None
    for t0 in range(0, tm, tb):
        w0 = w.conv_w[0:COL_BLOCKS, :]
        if token is not None:
            w0 = w0 + pltpu.bitcast(_zero_of(token), f32)
        acc = sc.s2[pl.ds((t0 + HIST_SKIP) * COL_BLOCKS, tb * COL_BLOCKS), :] * jnp.tile(w0, (tb, 1))
        for k in range(1, CONV_WIDTH):
            wk = jnp.tile(w.conv_w[k * COL_BLOCKS:(k + 1) * COL_BLOCKS, :], (tb, 1))
            acc = acc + sc.s2[pl.ds((t0 + k + HIST_SKIP) * COL_BLOCKS, tb * COL_BLOCKS), :] * wk
        sc.d2[pl.ds(t0 * COL_BLOCKS, tb * COL_BLOCKS), :] = acc
        token = acc[0:COL_BLOCKS, :]
        for i in range(1, tb):
            token = token + acc[i * COL_BLOCKS:(i + 1) * COL_BLOCKS, :]
    hist_ref[0] = sc.s2[pl.ds((tm + HIST_SKIP) * COL_BLOCKS, CONV_HIST * COL_BLOCKS), :]
    sc.s2[pl.ds(0, HIST_ROWS), :] = sc.s2[pl.ds(tm * COL_BLOCKS, HIST_ROWS), :]

    _in_proj(w, sc, sc.zgm, Z_GATE_B, 3)

    ln_v_g = w.ln_v_g[...]
    ln_v_b = w.ln_v_b[...]
    for r in range(0, tm, ROW_CHUNK):
        rows = pl.ds(r, ROW_CHUNK)
        vn = _layernorm(_gelu(sc.za[rows, D:2 * D]), ln_v_g, ln_v_b)
        if vn_out_ref is not None:
            vn_out_ref[0, rows, :] = vn
        sc.vn[rows, :] = vn.astype(bf16)

    if tm % GMLP_CHUNK == 0:
        n_chunks = tm // GMLP_CHUNK
        for hd in range(N_HEADS):
            cols = slice(hd * HEAD_DIM, (hd + 1) * HEAD_DIM)
            rhs = jnp.concatenate(
                [sc.vn[n * GMLP_CHUNK:(n + 1) * GMLP_CHUNK, cols] for n in range(n_chunks)], axis=1)
            s = jnp.dot(w.wm[hd], rhs, preferred_element_type=f32)
            for n in range(n_chunks):
                sc.sp[n * GMLP_CHUNK:(n + 1) * GMLP_CHUNK, cols] = (
                    s[:, n * HEAD_DIM:(n + 1) * HEAD_DIM] + w.bsp[:, cols])
    else:
        rhs_ref[...] = jnp.zeros(rhs_ref.shape, bf16)
        rhs_ref[0:tm, :] = sc.vn[...]
        for hd in range(N_HEADS):
            cols = slice(hd * HEAD_DIM, (hd + 1) * HEAD_DIM)
            s = jnp.dot(w.wm[hd, 0:tm, :], rhs_ref[:, cols], preferred_element_type=f32)
            sc.sp[:, cols] = s + w.bsp[0:tm, cols]

    if next_head is not None:
        next_head()

    conv_b = w.conv_b[...]
    ln_c_g = w.ln_c_g[...]
    ln_c_b = w.ln_c_b[...]
    for r in range(0, tm, ROW_CHUNK):
        dw = jnp.concatenate(
            [sc.d2[pl.ds(r * COL_BLOCKS + c, ROW_CHUNK, stride=COL_BLOCKS), :] for c in range(COL_BLOCKS)],
            axis=1) + conv_b
        yb = _silu(_layernorm(dw, ln_c_g, ln_c_b)) * _silu(sc.zgm[pl.ds(r, ROW_CHUNK), 0:D])
        sc.actb[pl.ds(r, ROW_CHUNK), :] = yb.astype(bf16)
    sc.yb[...] = jnp.dot(sc.actb[...], w.w_o_b[...], preferred_element_type=f32)

    for r in range(0, tm, ROW_CHUNK):
        rows = pl.ds(r, ROW_CHUNK)
        ya = _gelu(sc.za[rows, 0:D]) * sc.sp[rows, :] * _silu(sc.za[rows, 2 * D:3 * D])
        sc.acta[rows, :] = ya.astype(bf16)
    sc.ya[...] = jnp.dot(sc.acta[...], w.w_o_a[...], preferred_element_type=f32)

    for r in range(0, tm, ROW_CHUNK):
        rows = pl.ds(r, ROW_CHUNK)
        m = (_sigmoid(sc.zgm[rows, D:2 * D]) * sc.ya[rows, :]
             + _sigmoid(sc.zgm[rows, 2 * D:3 * D]) * sc.yb[rows, :])
        sc.actm[rows, :] = m.astype(bf16)
    sc.o[...] = jnp.dot(sc.actm[...], w.w_out[...], preferred_element_type=f32)
    gate = mod_ref[0, :, 2 * D:3 * D]
    final_g = w.final_g[...]
    for r in range(0, tm, ROW_CHUNK):
        rows = pl.ds(r, ROW_CHUNK)
        xo = x_ref[0, rows, :] + gate * sc.o[rows, :]
        y_ref[0, rows, :] = (xo * _rms_scale(xo)) * final_g


def _prompt_body(tm, *refs):
    x_ref, mod_ref, xn_ref, modn_ref = refs[0:4]
    w = _Weights(refs[4:4 + N_WEIGHTS])
    y_ref, hist_ref = refs[4 + N_WEIGHTS:6 + N_WEIGHTS]
    sc = _Scratch(refs[6 + N_WEIGHTS:])
    b = pl.program_id(0)
    t = pl.program_id(1)

    @pl.when(jnp.logical_and(b == 0, t == 0))
    def _():
        _tile_head(tm, x_ref, mod_ref, w, sc)

    @pl.when(t == 0)
    def _():
        sc.s2[pl.ds(0, HIST_ROWS), :] = jnp.zeros((HIST_ROWS, LANES), jnp.float32)

    next_head = functools.partial(_tile_head, tm, xn_ref, modn_ref, w, sc)
    _tile_body(tm, x_ref, mod_ref, next_head, w, sc, y_ref, hist_ref, None, None)


def _sample_body(tm, *refs):
    x_ref, mod_ref, state_ref = refs[0:3]
    w = _Weights(refs[3:3 + N_WEIGHTS])
    y_ref, hist_ref, vn_out_ref, rhs_ref = refs[3 + N_WEIGHTS:7 + N_WEIGHTS]
    sc = _Scratch(refs[7 + N_WEIGHTS:])
    sc.s2[pl.ds(0, HIST_SKIP * COL_BLOCKS), :] = jnp.zeros((HIST_SKIP * COL_BLOCKS, LANES), jnp.float32)
    sc.s2[pl.ds(HIST_SKIP * COL_BLOCKS, CONV_HIST * COL_BLOCKS), :] = state_ref[0]
    _tile_head(tm, x_ref, mod_ref, w, sc)
    _tile_body(tm, x_ref, mod_ref, None, w, sc, y_ref, hist_ref, vn_out_ref, rhs_ref)


def _resident(shape, n_grid_axes):
    zeros = (0,) * len(shape)
    if n_grid_axes == 1:
        return pl.BlockSpec(shape, lambda b: zeros, pipeline_mode=pl.Buffered(1))
    return pl.BlockSpec(shape, lambda b, t: zeros, pipeline_mode=pl.Buffered(1))


def _prompt_call(x, mod, weights, tm):
    bsz, tlen, _ = x.shape
    assert tlen % tm == 0 and tm % GMLP_CHUNK == 0
    nt = tlen // tm
    n_tiles = bsz * nt
    hist_shape = (1, CONV_HIST * COL_BLOCKS, LANES)

    def following(b, t):
        j = jnp.minimum(b * nt + t + 1, n_tiles - 1)
        return j // nt, j % nt

    in_specs = [pl.BlockSpec((1, tm, D), lambda b, t: (b, t, 0)),
                pl.BlockSpec((1, 1, 3 * D), lambda b, t: (b, 0, 0)),
                pl.BlockSpec((1, tm, D), lambda b, t: (*following(b, t), 0)),
                pl.BlockSpec((1, 1, 3 * D), lambda b, t: (following(b, t)[0], 0, 0))]
    in_specs += [_resident(p.shape, 2) for p in weights]
    out_specs = [pl.BlockSpec((1, tm, D), lambda b, t: (b, t, 0)),
                 pl.BlockSpec(hist_shape, lambda b, t: (b, 0, 0))]
    out_shape = [jax.ShapeDtypeStruct((bsz, tlen, D), jnp.float32),
                 jax.ShapeDtypeStruct((bsz,) + hist_shape[1:], jnp.float32)]
    return pl.pallas_call(
        functools.partial(_prompt_body, tm),
        grid=(bsz, nt),
        in_specs=in_specs,
        out_specs=out_specs,
        out_shape=out_shape,
        scratch_shapes=_tile_scratch(tm),
        compiler_params=pltpu.CompilerParams(
            dimension_semantics=("arbitrary", "arbitrary"),
            vmem_limit_bytes=VMEM_LIMIT_BYTES),
        name="prompt_layer",
    )(x, mod, x, mod, *weights)


def _sample_call(x, mod, state2d, weights):
    bsz, tm, _ = x.shape
    assert tm % ROW_CHUNK == 0 and tm < GMLP_CHUNK
    hist_shape = (1, CONV_HIST * COL_BLOCKS, LANES)
    in_specs = [pl.BlockSpec((1, tm, D), lambda b: (b, 0, 0)),
                pl.BlockSpec((1, 1, 3 * D), lambda b: (b, 0, 0)),
                pl.BlockSpec(hist_shape, lambda b: (b, 0, 0))]
    in_specs += [_resident(p.shape, 1) for p in weights]
    out_specs = [pl.BlockSpec((1, tm, D), lambda b: (b, 0, 0)),
                 pl.BlockSpec(hist_shape, lambda b: (b, 0, 0)),
                 pl.BlockSpec((1, tm, D), lambda b: (b, 0, 0))]
    out_shape = [jax.ShapeDtypeStruct((bsz, tm, D), jnp.float32),
                 jax.ShapeDtypeStruct((bsz,) + hist_shape[1:], jnp.float32),
                 jax.ShapeDtypeStruct((bsz, tm, D), jnp.float32)]
    scratch = [pltpu.VMEM((GMLP_CHUNK, D), jnp.bfloat16)]
    scratch += _tile_scratch(tm)
    return pl.pallas_call(
        functools.partial(_sample_body, tm),
        grid=(bsz,),
        in_specs=in_specs,
        out_specs=out_specs,
        out_shape=out_shape,
        scratch_shapes=scratch,
        compiler_params=pltpu.CompilerParams(
            dimension_semantics=("arbitrary",),
            vmem_limit_bytes=VMEM_LIMIT_BYTES),
        name="sample_layer",
    )(x, mod, state2d, *weights)


def kernel(x_prompt, x_sample, state_conv, c_prompt, c_sample, w_ada, b_ada, norm_g, w_in, b_in, ln_v_g, ln_v_b, w_spatial, b_spatial, conv_w, conv_b, ln_c_g, ln_c_b, w_o_a, w_o_b, w_out, final_g):
    assert w_ada.shape[0] == 1, "single-layer kernel"
    bf16 = jnp.bfloat16
    n_prompt = x_prompt.shape[0]
    n_sample = x_sample.shape[0]

    mod = _adaln(jnp.concatenate([c_prompt, c_sample], axis=0), w_ada[0], b_ada[0][None])
    mod = mod[:, None, :]

    tril = jnp.tril(jnp.ones((GMLP_CHUNK, GMLP_CHUNK), dtype=bool))
    wm = jnp.where(tril[None], w_spatial[0], 0.0).astype(bf16)
    bsp = jnp.repeat(b_spatial[0].T, HEAD_DIM, axis=1)
    row = lambda a: a.reshape(1, -1)
    weights = (row(norm_g[0]), w_in[0].astype(bf16), row(b_in[0]), row(ln_v_g[0]), row(ln_v_b[0]), wm, bsp,
               conv_w[0].reshape(CONV_WIDTH * COL_BLOCKS, LANES), row(conv_b[0]), row(ln_c_g[0]), row(ln_c_b[0]),
               w_o_a[0].astype(bf16), w_o_b[0].astype(bf16), w_out[0].astype(bf16), row(final_g))
    assert len(weights) == N_WEIGHTS

    y_prompt, hist_prompt = _prompt_call(x_prompt, mod[:n_prompt], weights, PROMPT_TILE)
    state2d = state_conv[0].reshape(n_sample, CONV_HIST * COL_BLOCKS, LANES)
    y_sample, hist_sample, vn_sample = _sample_call(x_sample, mod[n_prompt:], state2d, weights)

    new_conv_prompt = hist_prompt.reshape(1, n_prompt, CONV_HIST, D)
    new_conv_sample = hist_sample.reshape(1, n_sample, CONV_HIST, D)
    return (y_prompt, y_sample, new_conv_prompt, new_conv_sample, vn_sample[None])
```

```python
import functools

import jax
import jax.numpy as jnp
from jax import lax
from jax.experimental import pallas as pl
from jax.experimental.pallas import tpu as pltpu

D = 1024
N_IN_BLOCKS = 8
GMLP_CHUNK = 128
N_HEADS = 8
HEAD_DIM = D // N_HEADS
CONV_WIDTH = 31
CONV_HIST = CONV_WIDTH - 1
EPS = 1e-6

LANES = 128
SUBLANES = 8
COL_BLOCKS = D // LANES
HIST_PAD = 32
HIST_SKIP = HIST_PAD - CONV_HIST
HIST_ROWS = HIST_PAD * COL_BLOCKS
ROW_CHUNK = 16
CONV_TBLOCK = 4
PROMPT_TILE = 256
VMEM_LIMIT_BYTES = 56 * 1024 * 1024

Z_U, Z_V, Z_GATE_A, Z_GLU_A, Z_GLU_B, Z_GATE_B, Z_MERGE_A, Z_MERGE_B = range(N_IN_BLOCKS)

HALVED_BLOCKS = (Z_GATE_A, Z_GLU_B, Z_GATE_B, Z_MERGE_A, Z_MERGE_B)

_GELU_C = 0.7978845608028654
_GELU_K = 0.044715


def _gelu(x):
    half = 0.5 * x
    return half + half * jnp.tanh(x * (_GELU_C + (_GELU_C * _GELU_K) * (x * x)))


def _sigmoid_of_double(xh):
    return 0.5 * jnp.tanh(xh) + 0.5


def _silu_of_double(xh):
    return xh * jnp.tanh(xh) + xh


def _silu(x):
    return _silu_of_double(0.5 * x)


def _layernorm(x, g, b):
    mu = jnp.mean(x, axis=-1, keepdims=True)
    xc = x - mu
    var = jnp.mean(xc * xc, axis=-1, keepdims=True)
    return xc * lax.rsqrt(var + EPS) * g + b


def _rms_scale(x):
    return lax.rsqrt(jnp.mean(x * x, axis=-1, keepdims=True) + EPS)


def _zero_of(token):
    return (pltpu.bitcast(token, jnp.uint32) >> 16) >> 16


def _adaln_body(c_ref, w_ref, b_ref, o_ref):
    a = _silu(c_ref[...]).astype(jnp.bfloat16)
    o_ref[...] = jnp.dot(a, w_ref[...].astype(jnp.bfloat16),
                         preferred_element_type=jnp.float32) + b_ref[...]


def _adaln(c_all, w_ada, b_ada):
    n = c_all.shape[0]
    blk = 3 * D // 4
    return pl.pallas_call(
        _adaln_body,
        grid=(4,),
        in_specs=[pl.BlockSpec((n, D), lambda j: (0, 0)),
                  pl.BlockSpec((D, blk), lambda j: (0, j)),
                  pl.BlockSpec((1, blk), lambda j: (0, j))],
        out_specs=pl.BlockSpec((n, blk), lambda j: (0, j)),
        out_shape=jax.ShapeDtypeStruct((n, 3 * D), jnp.float32),
        name="adaln",
    )(c_all, w_ada, b_ada)


class _Weights:
    def __init__(self, refs):
        (self.norm_g, self.w_in, self.b_in, self.ln_v_g, self.ln_v_b, self.wm, self.bsp, self.conv_w,
         self.conv_b, self.ln_c_g, self.ln_c_b, self.w_o_a, self.w_o_b, self.w_out, self.final_g) = refs


N_WEIGHTS = 15


class _Scratch:
    def __init__(self, refs):
        (self.hb, self.zglu, self.za, self.zgm, self.vn, self.sp, self.acta, self.actb, self.actm,
         self.ya, self.yb, self.o, self.s2, self.d2) = refs


def _tile_scratch(tm):
    f32, bf16 = jnp.float32, jnp.bfloat16
    return [
        pltpu.VMEM((tm, D), bf16),
        pltpu.VMEM((tm, 2 * D), f32),
        pltpu.VMEM((tm, 3 * D), f32),
        pltpu.VMEM((tm, 3 * D), f32),
        pltpu.VMEM((tm, D), bf16),
        pltpu.VMEM((tm, D), f32),
        pltpu.VMEM((tm, D), bf16),
        pltpu.VMEM((tm, D), bf16),
        pltpu.VMEM((tm, D), bf16),
        pltpu.VMEM((tm, D), f32),
        pltpu.VMEM((tm, D), f32),
        pltpu.VMEM((tm, D), f32),
        pltpu.VMEM(((tm + HIST_PAD) * COL_BLOCKS, LANES), f32),
        pltpu.VMEM((tm * COL_BLOCKS, LANES), f32),
    ]


def _in_proj(w, sc, dst_ref, blk0, nblk):
    cols = slice(blk0 * D, (blk0 + nblk) * D)
    z = jnp.dot(sc.hb[...], w.w_in[:, cols], preferred_element_type=jnp.float32) + w.b_in[:, cols]
    dst_ref[...] = z
    return z[0:SUBLANES, 0:LANES]


def _tile_head(tm, x_ref, mod_ref, w, sc):
    shift = mod_ref[0, :, 0:D]
    gs = w.norm_g[...] * (1.0 + mod_ref[0, :, D:2 * D])
    for r in range(0, tm, ROW_CHUNK):
        x = x_ref[0, pl.ds(r, ROW_CHUNK), :]
        h = (x * _rms_scale(x)) * gs + shift
        sc.hb[pl.ds(r, ROW_CHUNK), :] = h.astype(jnp.bfloat16)
    return _in_proj(w, sc, sc.zglu, Z_GLU_A, 2)


def _tile_body(tm, x_ref, mod_ref, next_head, w, sc, y_ref, hist_ref, vn_out_ref, rhs_ref):
    f32, bf16 = jnp.float32, jnp.bfloat16

    for c in range(COL_BLOCKS):
        a = sc.zglu[:, c * LANES:(c + 1) * LANES]
        b = sc.zglu[:, D + c * LANES:D + (c + 1) * LANES]
        sc.s2[pl.ds(HIST_ROWS + c, tm, stride=COL_BLOCKS), :] = a * _sigmoid_of_double(b)

    _in_proj(w, sc, sc.za, Z_U, 3)
    _in_proj(w, sc, sc.zgm, Z_GATE_B, 3)

    head_done = next_head() if next_head is not None else None

    tb = min(CONV_TBLOCK, tm)
    token = None
    for t0 in range(0, tm, tb):
        w0 = w.conv_w[0:COL_BLOCKS, :]
        if token is not None:
            w0 = w0 + pltpu.bitcast(_zero_of(token), f32)
        acc = sc.s2[pl.ds((t0 + HIST_SKIP) * COL_BLOCKS, tb * COL_BLOCKS), :] * jnp.tile(w0, (tb, 1))
        for k in range(1, CONV_WIDTH):
            wk = jnp.tile(w.conv_w[k * COL_BLOCKS:(k + 1) * COL_BLOCKS, :], (tb, 1))
            acc = acc + sc.s2[pl.ds((t0 + k + HIST_SKIP) * COL_BLOCKS, tb * COL_BLOCKS), :] * wk
        sc.d2[pl.ds(t0 * COL_BLOCKS, tb * COL_BLOCKS), :] = acc
        token = acc[0:COL_BLOCKS, :]
        for i in range(1, tb):
            token = token + acc[i * COL_BLOCKS:(i + 1) * COL_BLOCKS, :]
    hist_ref[0] = sc.s2[pl.ds((tm + HIST_SKIP) * COL_BLOCKS, CONV_HIST * COL_BLOCKS), :]
    sc.s2[pl.ds(0, HIST_ROWS), :] = sc.s2[pl.ds(tm * COL_BLOCKS, HIST_ROWS), :]

    ln_v_g = w.ln_v_g[...]
    ln_v_b = w.ln_v_b[...]
    for r in range(0, tm, ROW_CHUNK):
        rows = pl.ds(r, ROW_CHUNK)
        vn = _layernorm(_gelu(sc.za[rows, D:2 * D]), ln_v_g, ln_v_b)
        if vn_out_ref is not None:
            vn_out_ref[0, rows, :] = vn
        sc.vn[rows, :] = vn.astype(bf16)

    if tm % GMLP_CHUNK == 0:
        n_chunks = tm // GMLP_CHUNK
        for hd in range(N_HEADS):
            cols = slice(hd * HEAD_DIM, (hd + 1) * HEAD_DIM)
            rhs = jnp.concatenate(
                [sc.vn[n * GMLP_CHUNK:(n + 1) * GMLP_CHUNK, cols] for n in range(n_chunks)], axis=1)
            s = jnp.dot(w.wm[hd], rhs, preferred_element_type=f32)
            for n in range(n_chunks):
                sc.sp[n * GMLP_CHUNK:(n + 1) * GMLP_CHUNK, cols] = (
                    s[:, n * HEAD_DIM:(n + 1) * HEAD_DIM] + w.bsp[:, cols])
    else:
        rhs_ref[...] = jnp.zeros(rhs_ref.shape, bf16)
        rhs_ref[0:tm, :] = sc.vn[...]
        for hd in range(N_HEADS):
            cols = slice(hd * HEAD_DIM, (hd + 1) * HEAD_DIM)
            s = jnp.dot(w.wm[hd, 0:tm, :], rhs_ref[:, cols], preferred_element_type=f32)
            sc.sp[:, cols] = s + w.bsp[0:tm, cols]

    conv_b = w.conv_b[...]
    if head_done is not None:
        zero_row = pltpu.bitcast(_zero_of(head_done), f32)[0:1, :]
        conv_b = conv_b + jnp.tile(zero_row, (1, COL_BLOCKS))
    ln_c_g = w.ln_c_g[...]
    ln_c_b = w.ln_c_b[...]
    for r in range(0, tm, ROW_CHUNK):
        dw = jnp.concatenate(
            [sc.d2[pl.ds(r * COL_BLOCKS + c, ROW_CHUNK, stride=COL_BLOCKS), :] for c in range(COL_BLOCKS)],
            axis=1) + conv_b
        yb = _silu(_layernorm(dw, ln_c_g, ln_c_b)) * _silu_of_double(sc.zgm[pl.ds(r, ROW_CHUNK), 0:D])
        sc.actb[pl.ds(r, ROW_CHUNK), :] = yb.astype(bf16)
    sc.yb[...] = jnp.dot(sc.actb[...], w.w_o_b[...], preferred_element_type=f32)

    for r in range(0, tm, ROW_CHUNK):
        rows = pl.ds(r, ROW_CHUNK)
        ya = _gelu(sc.za[rows, 0:D]) * sc.sp[rows, :] * _silu_of_double(sc.za[rows, 2 * D:3 * D])
        sc.acta[rows, :] = ya.astype(bf16)
    sc.ya[...] = jnp.dot(sc.acta[...], w.w_o_a[...], preferred_element_type=f32)

    for r in range(0, tm, ROW_CHUNK):
        rows = pl.ds(r, ROW_CHUNK)
        m = (_sigmoid_of_double(sc.zgm[rows, D:2 * D]) * sc.ya[rows, :]
             + _sigmoid_of_double(sc.zgm[rows, 2 * D:3 * D]) * sc.yb[rows, :])
        sc.actm[rows, :] = m.astype(bf16)
    sc.o[...] = jnp.dot(sc.actm[...], w.w_out[...], preferred_element_type=f32)
    gate = mod_ref[0, :, 2 * D:3 * D]
    final_g = w.final_g[...]
    for r in range(0, tm, ROW_CHUNK):
        rows = pl.ds(r, ROW_CHUNK)
        xo = x_ref[0, rows, :] + gate * sc.o[rows, :]
        y_ref[0, rows, :] = (xo * _rms_scale(xo)) * final_g


def _prompt_body(tm, *refs):
    x_ref, mod_ref, xn_ref, modn_ref = refs[0:4]
    w = _Weights(refs[4:4 + N_WEIGHTS])
    y_ref, hist_ref = refs[4 + N_WEIGHTS:6 + N_WEIGHTS]
    sc = _Scratch(refs[6 + N_WEIGHTS:])
    b = pl.program_id(0)
    t = pl.program_id(1)

    @pl.when(jnp.logical_and(b == 0, t == 0))
    def _():
        _tile_head(tm, x_ref, mod_ref, w, sc)

    @pl.when(t == 0)
    def _():
        sc.s2[pl.ds(0, HIST_ROWS), :] = jnp.zeros((HIST_ROWS, LANES), jnp.float32)

    next_head = functools.partial(_tile_head, tm, xn_ref, modn_ref, w, sc)
    _tile_body(tm, x_ref, mod_ref, next_head, w, sc, y_ref, hist_ref, None, None)


def _sample_body(tm, *refs):
    x_ref, mod_ref, state_ref = refs[0:3]
    w = _Weights(refs[3:3 + N_WEIGHTS])
    y_ref, hist_ref, vn_out_ref, rhs_ref = refs[3 + N_WEIGHTS:7 + N_WEIGHTS]
    sc = _Scratch(refs[7 + N_WEIGHTS:])
    sc.s2[pl.ds(0, HIST_SKIP * COL_BLOCKS), :] = jnp.zeros((HIST_SKIP * COL_BLOCKS, LANES), jnp.float32)
    sc.s2[pl.ds(HIST_SKIP * COL_BLOCKS, CONV_HIST * COL_BLOCKS), :] = state_ref[0]
    _tile_head(tm, x_ref, mod_ref, w, sc)
    _tile_body(tm, x_ref, mod_ref, None, w, sc, y_ref, hist_ref, vn_out_ref, rhs_ref)


def _resident(shape, n_grid_axes):
    zeros = (0,) * len(shape)
    if n_grid_axes == 1:
        return pl.BlockSpec(shape, lambda b: zeros, pipeline_mode=pl.Buffered(1))
    return pl.BlockSpec(shape, lambda b, t: zeros, pipeline_mode=pl.Buffered(1))


def _prompt_call(x, mod, weights, tm):
    bsz, tlen, _ = x.shape
    assert tlen % tm == 0 and tm % GMLP_CHUNK == 0
    nt = tlen // tm
    n_tiles = bsz * nt
    hist_shape = (1, CONV_HIST * COL_BLOCKS, LANES)

    def following(b, t):
        j = jnp.minimum(b * nt + t + 1, n_tiles - 1)
        return j // nt, j % nt

    in_specs = [pl.BlockSpec((1, tm, D), lambda b, t: (b, t, 0)),
                pl.BlockSpec((1, 1, 3 * D), lambda b, t: (b, 0, 0)),
                pl.BlockSpec((1, tm, D), lambda b, t: (*following(b, t), 0)),
                pl.BlockSpec((1, 1, 3 * D), lambda b, t: (following(b, t)[0], 0, 0))]
    in_specs += [_resident(p.shape, 2) for p in weights]
    out_specs = [pl.BlockSpec((1, tm, D), lambda b, t: (b, t, 0)),
                 pl.BlockSpec(hist_shape, lambda b, t: (b, 0, 0))]
    out_shape = [jax.ShapeDtypeStruct((bsz, tlen, D), jnp.float32),
                 jax.ShapeDtypeStruct((bsz,) + hist_shape[1:], jnp.float32)]
    return pl.pallas_call(
        functools.partial(_prompt_body, tm),
        grid=(bsz, nt),
        in_specs=in_specs,
        out_specs=out_specs,
        out_shape=out_shape,
        scratch_shapes=_tile_scratch(tm),
        compiler_params=pltpu.CompilerParams(
            dimension_semantics=("arbitrary", "arbitrary"),
            vmem_limit_bytes=VMEM_LIMIT_BYTES),
        name="prompt_layer",
    )(x, mod, x, mod, *weights)


def _sample_call(x, mod, state2d, weights):
    bsz, tm, _ = x.shape
    assert tm % ROW_CHUNK == 0 and tm < GMLP_CHUNK
    hist_shape = (1, CONV_HIST * COL_BLOCKS, LANES)
    in_specs = [pl.BlockSpec((1, tm, D), lambda b: (b, 0, 0)),
                pl.BlockSpec((1, 1, 3 * D), lambda b: (b, 0, 0)),
                pl.BlockSpec(hist_shape, lambda b: (b, 0, 0))]
    in_specs += [_resident(p.shape, 1) for p in weights]
    out_specs = [pl.BlockSpec((1, tm, D), lambda b: (b, 0, 0)),
                 pl.BlockSpec(hist_shape, lambda b: (b, 0, 0)),
                 pl.BlockSpec((1, tm, D), lambda b: (b, 0, 0))]
    out_shape = [jax.ShapeDtypeStruct((bsz, tm, D), jnp.float32),
                 jax.ShapeDtypeStruct((bsz,) + hist_shape[1:], jnp.float32),
                 jax.ShapeDtypeStruct((bsz, tm, D), jnp.float32)]
    scratch = [pltpu.VMEM((GMLP_CHUNK, D), jnp.bfloat16)]
    scratch += _tile_scratch(tm)
    return pl.pallas_call(
        functools.partial(_sample_body, tm),
        grid=(bsz,),
        in_specs=in_specs,
        out_specs=out_specs,
        out_shape=out_shape,
        scratch_shapes=scratch,
        compiler_params=pltpu.CompilerParams(
            dimension_semantics=("arbitrary",),
            vmem_limit_bytes=VMEM_LIMIT_BYTES),
        name="sample_layer",
    )(x, mod, state2d, *weights)


def kernel(x_prompt, x_sample, state_conv, c_prompt, c_sample, w_ada, b_ada, norm_g, w_in, b_in, ln_v_g, ln_v_b, w_spatial, b_spatial, conv_w, conv_b, ln_c_g, ln_c_b, w_o_a, w_o_b, w_out, final_g):
    assert w_ada.shape[0] == 1, "single-layer kernel"
    bf16 = jnp.bfloat16
    n_prompt = x_prompt.shape[0]
    n_sample = x_sample.shape[0]

    mod = _adaln(jnp.concatenate([c_prompt, c_sample], axis=0), w_ada[0], b_ada[0][None])
    mod = mod[:, None, :]

    tril = jnp.tril(jnp.ones((GMLP_CHUNK, GMLP_CHUNK), dtype=bool))
    wm = jnp.where(tril[None], w_spatial[0], 0.0).astype(bf16)
    bsp = jnp.repeat(b_spatial[0].T, HEAD_DIM, axis=1)
    row = lambda a: a.reshape(1, -1)
    halved = jnp.zeros((N_IN_BLOCKS,), bool).at[jnp.array(HALVED_BLOCKS)].set(True)
    in_scale = jnp.repeat(jnp.where(halved, 0.5, 1.0), D)[None, :]
    weights = (row(norm_g[0]), (w_in[0] * in_scale).astype(bf16), row(b_in[0]) * in_scale,
               row(ln_v_g[0]), row(ln_v_b[0]), wm, bsp,
               conv_w[0].reshape(CONV_WIDTH * COL_BLOCKS, LANES), row(conv_b[0]), row(ln_c_g[0]), row(ln_c_b[0]),
               w_o_a[0].astype(bf16), w_o_b[0].astype(bf16), w_out[0].astype(bf16), row(final_g))
    assert len(weights) == N_WEIGHTS

    y_prompt, hist_prompt = _prompt_call(x_prompt, mod[:n_prompt], weights, PROMPT_TILE)
    state2d = state_conv[0].reshape(n_sample, CONV_HIST * COL_BLOCKS, LANES)
    y_sample, hist_sample, vn_sample = _sample_call(x_sample, mod[n_prompt:], state2d, weights)

    new_conv_prompt = hist_prompt.reshape(1, n_prompt, CONV_HIST, D)
    new_conv_sample = hist_sample.reshape(1, n_sample, CONV_HIST, D)
    return (y_prompt, y_sample, new_conv_prompt, new_conv_sample, vn_sample[None])
```

```python
import functools

import jax
import jax.numpy as jnp
from jax import lax
from jax.experimental import pallas as pl
from jax.experimental.pallas import tpu as pltpu

D = 1024
N_IN_BLOCKS = 8
GMLP_CHUNK = 128
N_HEADS = 8
HEAD_DIM = D // N_HEADS
CONV_WIDTH = 31
CONV_HIST = CONV_WIDTH - 1
EPS = 1e-6

LANES = 128
SUBLANES = 8
COL_BLOCKS = D // LANES
HIST_PAD = 32
HIST_SKIP = HIST_PAD - CONV_HIST
HIST_ROWS = HIST_PAD * COL_BLOCKS
ROW_CHUNK = 16
CONV_TBLOCK = 4
PROMPT_TILE = 256
VMEM_LIMIT_BYTES = 56 * 1024 * 1024

Z_U, Z_V, Z_GATE_A, Z_GLU_A, Z_GLU_B, Z_GATE_B, Z_MERGE_A, Z_MERGE_B = range(N_IN_BLOCKS)

HALVED_BLOCKS = (Z_GATE_A, Z_GLU_B, Z_GATE_B, Z_MERGE_A, Z_MERGE_B)

_GELU_C = 0.7978845608028654
_GELU_K = 0.044715


def _gelu(x):
    half = 0.5 * x
    return half + half * jnp.tanh(x * (_GELU_C + (_GELU_C * _GELU_K) * (x * x)))


def _sigmoid_of_double(xh):
    return 0.5 * jnp.tanh(xh) + 0.5


def _silu_of_double(xh):
    return xh * jnp.tanh(xh) + xh


def _silu(x):
    return _silu_of_double(0.5 * x)


def _layernorm(x, g, b):
    mu = jnp.mean(x, axis=-1, keepdims=True)
    xc = x - mu
    var = jnp.mean(xc * xc, axis=-1, keepdims=True)
    return xc * lax.rsqrt(var + EPS) * g + b


def _rms_scale(x):
    return lax.rsqrt(jnp.mean(x * x, axis=-1, keepdims=True) + EPS)


def _zero_of(token):
    return (pltpu.bitcast(token, jnp.uint32) >> 16) >> 16


def _adaln_body(c_ref, w_ref, b_ref, o_ref):
    a = _silu(c_ref[...]).astype(jnp.bfloat16)
    o_ref[...] = jnp.dot(a, w_ref[...].astype(jnp.bfloat16),
                         preferred_element_type=jnp.float32) + b_ref[...]


def _adaln(c_all, w_ada, b_ada):
    n = c_all.shape[0]
    blk = 3 * D // 4
    return pl.pallas_call(
        _adaln_body,
        grid=(4,),
        in_specs=[pl.BlockSpec((n, D), lambda j: (0, 0)),
                  pl.BlockSpec((D, blk), lambda j: (0, j)),
                  pl.BlockSpec((1, blk), lambda j: (0, j))],
        out_specs=pl.BlockSpec((n, blk), lambda j: (0, j)),
        out_shape=jax.ShapeDtypeStruct((n, 3 * D), jnp.float32),
        name="adaln",
    )(c_all, w_ada, b_ada)


class _Weights:
    def __init__(self, refs):
        (self.norm_g, self.w_in, self.b_in, self.ln_v_g, self.ln_v_b, self.wm, self.bsp, self.conv_w,
         self.conv_b, self.ln_c_g, self.ln_c_b, self.w_o_a, self.w_o_b, self.w_out, self.final_g) = refs


N_WEIGHTS = 15


class _Scratch:
    def __init__(self, refs):
        (self.hb, self.zglu, self.za, self.zgm, self.vn, self.sp, self.acta, self.actb, self.actm,
         self.ya, self.yb, self.o, self.s2, self.d2) = refs


def _tile_scratch(n_streams, tlen):
    f32, bf16 = jnp.float32, jnp.bfloat16
    tm = n_streams * tlen
    return [
        pltpu.VMEM((tm, D), bf16),
        pltpu.VMEM((tm, 2 * D), f32),
        pltpu.VMEM((tm, 3 * D), f32),
        pltpu.VMEM((tm, 3 * D), f32),
        pltpu.VMEM((tm, D), bf16),
        pltpu.VMEM((tm, D), f32),
        pltpu.VMEM((tm, D), bf16),
        pltpu.VMEM((tm, D), bf16),
        pltpu.VMEM((tm, D), bf16),
        pltpu.VMEM((tm, D), f32),
        pltpu.VMEM((tm, D), f32),
        pltpu.VMEM((tm, D), f32),
        pltpu.VMEM((n_streams * (tlen + HIST_PAD) * COL_BLOCKS, LANES), f32),
        pltpu.VMEM((tm * COL_BLOCKS, LANES), f32),
    ]


def _in_proj(w, sc, dst_ref, blk0, nblk):
    cols = slice(blk0 * D, (blk0 + nblk) * D)
    z = jnp.dot(sc.hb[...], w.w_in[:, cols], preferred_element_type=jnp.float32) + w.b_in[:, cols]
    dst_ref[...] = z
    return z[0:SUBLANES, 0:LANES]


def _tile_head(n_streams, tlen, x_ref, mod_ref, w, sc):
    for s in range(n_streams):
        shift = mod_ref[s, :, 0:D]
        gs = w.norm_g[...] * (1.0 + mod_ref[s, :, D:2 * D])
        for r in range(0, tlen, ROW_CHUNK):
            x = x_ref[s, pl.ds(r, ROW_CHUNK), :]
            h = (x * _rms_scale(x)) * gs + shift
            sc.hb[pl.ds(s * tlen + r, ROW_CHUNK), :] = h.astype(jnp.bfloat16)
    return _in_proj(w, sc, sc.zglu, Z_GLU_A, 2)


def _s2_base(s, tlen):
    return s * (HIST_PAD + tlen) * COL_BLOCKS


def _tile_body(n_streams, tlen, x_ref, mod_ref, next_head, w, sc, y_ref, hist_ref, vn_out_ref, rhs_ref):
    f32, bf16 = jnp.float32, jnp.bfloat16
    tm = n_streams * tlen
    chunks = [(s, r) for s in range(n_streams) for r in range(0, tlen, ROW_CHUNK)]

    for s in range(n_streams):
        rows = pl.ds(s * tlen, tlen)
        for c in range(COL_BLOCKS):
            a = sc.zglu[rows, c * LANES:(c + 1) * LANES]
            b = sc.zglu[rows, D + c * LANES:D + (c + 1) * LANES]
            sc.s2[pl.ds(_s2_base(s, tlen) + HIST_ROWS + c, tlen, stride=COL_BLOCKS), :] = a * _sigmoid_of_double(b)

    _in_proj(w, sc, sc.za, Z_U, 3)
    _in_proj(w, sc, sc.zgm, Z_GATE_B, 3)

    head_done = next_head() if next_head is not None else None

    tb = min(CONV_TBLOCK, tlen)
    token = None
    for s in range(n_streams):
        base = _s2_base(s, tlen)
        for t0 in range(0, tlen, tb):
            w0 = w.conv_w[0:COL_BLOCKS, :]
            if token is not None:
                w0 = w0 + pltpu.bitcast(_zero_of(token), f32)
            acc = sc.s2[pl.ds(base + (t0 + HIST_SKIP) * COL_BLOCKS, tb * COL_BLOCKS), :] * jnp.tile(w0, (tb, 1))
            for k in range(1, CONV_WIDTH):
                wk = jnp.tile(w.conv_w[k * COL_BLOCKS:(k + 1) * COL_BLOCKS, :], (tb, 1))
                acc = acc + sc.s2[pl.ds(base + (t0 + k + HIST_SKIP) * COL_BLOCKS, tb * COL_BLOCKS), :] * wk
            sc.d2[pl.ds((s * tlen + t0) * COL_BLOCKS, tb * COL_BLOCKS), :] = acc
            token = acc[0:COL_BLOCKS, :]
            for i in range(1, tb):
                token = token + acc[i * COL_BLOCKS:(i + 1) * COL_BLOCKS, :]
        hist_ref[s] = sc.s2[pl.ds(base + (tlen + HIST_SKIP) * COL_BLOCKS, CONV_HIST * COL_BLOCKS), :]
        sc.s2[pl.ds(base, HIST_ROWS), :] = sc.s2[pl.ds(base + tlen * COL_BLOCKS, HIST_ROWS), :]

    ln_v_g = w.ln_v_g[...]
    ln_v_b = w.ln_v_b[...]
    for s, r in chunks:
        rows = pl.ds(s * tlen + r, ROW_CHUNK)
        vn = _layernorm(_gelu(sc.za[rows, D:2 * D]), ln_v_g, ln_v_b)
        if vn_out_ref is not None:
            vn_out_ref[s, pl.ds(r, ROW_CHUNK), :] = vn
        sc.vn[rows, :] = vn.astype(bf16)

    if tlen % GMLP_CHUNK == 0:
        n_chunks = tm // GMLP_CHUNK
        for hd in range(N_HEADS):
            cols = slice(hd * HEAD_DIM, (hd + 1) * HEAD_DIM)
            rhs = jnp.concatenate(
                [sc.vn[n * GMLP_CHUNK:(n + 1) * GMLP_CHUNK, cols] for n in range(n_chunks)], axis=1)
            sp = jnp.dot(w.wm[hd], rhs, preferred_element_type=f32)
            for n in range(n_chunks):
                sc.sp[n * GMLP_CHUNK:(n + 1) * GMLP_CHUNK, cols] = (
                    sp[:, n * HEAD_DIM:(n + 1) * HEAD_DIM] + w.bsp[:, cols])
    else:
        rhs_ref[...] = jnp.zeros(rhs_ref.shape, bf16)
        for hd in range(N_HEADS):
            cols = slice(hd * HEAD_DIM, (hd + 1) * HEAD_DIM)
            for s in range(n_streams):
                rhs_ref[0:tlen, s * HEAD_DIM:(s + 1) * HEAD_DIM] = sc.vn[pl.ds(s * tlen, tlen), cols]
            sp = jnp.dot(w.wm[hd, 0:tlen, :], rhs_ref[...], preferred_element_type=f32)
            for s in range(n_streams):
                sc.sp[pl.ds(s * tlen, tlen), cols] = sp[:, s * HEAD_DIM:(s + 1) * HEAD_DIM] + w.bsp[0:tlen, cols]

    conv_b = w.conv_b[...]
    if head_done is not None:
        zero_row = pltpu.bitcast(_zero_of(head_done), f32)[0:1, :]
        conv_b = conv_b + jnp.tile(zero_row, (1, COL_BLOCKS))
    ln_c_g = w.ln_c_g[...]
    ln_c_b = w.ln_c_b[...]
    for s, r in chunks:
        row0 = s * tlen + r
        dw = jnp.concatenate(
            [sc.d2[pl.ds(row0 * COL_BLOCKS + c, ROW_CHUNK, stride=COL_BLOCKS), :] for c in range(COL_BLOCKS)],
            axis=1) + conv_b
        yb = _silu(_layernorm(dw, ln_c_g, ln_c_b)) * _silu_of_double(sc.zgm[pl.ds(row0, ROW_CHUNK), 0:D])
        sc.actb[pl.ds(row0, ROW_CHUNK), :] = yb.astype(bf16)
    sc.yb[...] = jnp.dot(sc.actb[...], w.w_o_b[...], preferred_element_type=f32)

    for r in range(0, tm, ROW_CHUNK):
        rows = pl.ds(r, ROW_CHUNK)
        ya = _gelu(sc.za[rows, 0:D]) * sc.sp[rows, :] * _silu_of_double(sc.za[rows, 2 * D:3 * D])
        sc.acta[rows, :] = ya.astype(bf16)
    sc.ya[...] = jnp.dot(sc.acta[...], w.w_o_a[...], preferred_element_type=f32)

    for r in range(0, tm, ROW_CHUNK):
        rows = pl.ds(r, ROW_CHUNK)
        m = (_sigmoid_of_double(sc.zgm[rows, D:2 * D]) * sc.ya[rows, :]
             + _sigmoid_of_double(sc.zgm[rows, 2 * D:3 * D]) * sc.yb[rows, :])
        sc.actm[rows, :] = m.astype(bf16)
    sc.o[...] = jnp.dot(sc.actm[...], w.w_out[...], preferred_element_type=f32)
    final_g = w.final_g[...]
    for s, r in chunks:
        gate = mod_ref[s, :, 2 * D:3 * D]
        xo = x_ref[s, pl.ds(r, ROW_CHUNK), :] + gate * sc.o[pl.ds(s * tlen + r, ROW_CHUNK), :]
        y_ref[s, pl.ds(r, ROW_CHUNK), :] = (xo * _rms_scale(xo)) * final_g


def _prompt_body(tm, *refs):
    x_ref, mod_ref, xn_ref, modn_ref = refs[0:4]
    w = _Weights(refs[4:4 + N_WEIGHTS])
    y_ref, hist_ref = refs[4 + N_WEIGHTS:6 + N_WEIGHTS]
    sc = _Scratch(refs[6 + N_WEIGHTS:])
    b = pl.program_id(0)
    t = pl.program_id(1)

    @pl.when(jnp.logical_and(b == 0, t == 0))
    def _():
        _tile_head(1, tm, x_ref, mod_ref, w, sc)

    @pl.when(t == 0)
    def _():
        sc.s2[pl.ds(0, HIST_ROWS), :] = jnp.zeros((HIST_ROWS, LANES), jnp.float32)

    next_head = functools.partial(_tile_head, 1, tm, xn_ref, modn_ref, w, sc)
    _tile_body(1, tm, x_ref, mod_ref, next_head, w, sc, y_ref, hist_ref, None, None)


def _sample_body(n_streams, tlen, *refs):
    x_ref, mod_ref, state_ref = refs[0:3]
    w = _Weights(refs[3:3 + N_WEIGHTS])
    y_ref, hist_ref, vn_out_ref, rhs_ref = refs[3 + N_WEIGHTS:7 + N_WEIGHTS]
    sc = _Scratch(refs[7 + N_WEIGHTS:])
    for s in range(n_streams):
        base = _s2_base(s, tlen)
        sc.s2[pl.ds(base, HIST_SKIP * COL_BLOCKS), :] = jnp.zeros((HIST_SKIP * COL_BLOCKS, LANES), jnp.float32)
        sc.s2[pl.ds(base + HIST_SKIP * COL_BLOCKS, CONV_HIST * COL_BLOCKS), :] = state_ref[s]
    _tile_head(n_streams, tlen, x_ref, mod_ref, w, sc)
    _tile_body(n_streams, tlen, x_ref, mod_ref, None, w, sc, y_ref, hist_ref, vn_out_ref, rhs_ref)


def _resident(shape, n_grid_axes):
    zeros = (0,) * len(shape)
    if n_grid_axes == 1:
        return pl.BlockSpec(shape, lambda b: zeros, pipeline_mode=pl.Buffered(1))
    return pl.BlockSpec(shape, lambda b, t: zeros, pipeline_mode=pl.Buffered(1))


def _prompt_call(x, mod, weights, tm):
    bsz, tlen, _ = x.shape
    assert tlen % tm == 0 and tm % GMLP_CHUNK == 0
    nt = tlen // tm
    n_tiles = bsz * nt
    hist_shape = (1, CONV_HIST * COL_BLOCKS, LANES)

    def following(b, t):
        j = jnp.minimum(b * nt + t + 1, n_tiles - 1)
        return j // nt, j % nt

    in_specs = [pl.BlockSpec((1, tm, D), lambda b, t: (b, t, 0)),
                pl.BlockSpec((1, 1, 3 * D), lambda b, t: (b, 0, 0)),
                pl.BlockSpec((1, tm, D), lambda b, t: (*following(b, t), 0)),
                pl.BlockSpec((1, 1, 3 * D), lambda b, t: (following(b, t)[0], 0, 0))]
    in_specs += [_resident(p.shape, 2) for p in weights]
    out_specs = [pl.BlockSpec((1, tm, D), lambda b, t: (b, t, 0)),
                 pl.BlockSpec(hist_shape, lambda b, t: (b, 0, 0))]
    out_shape = [jax.ShapeDtypeStruct((bsz, tlen, D), jnp.float32),
                 jax.ShapeDtypeStruct((bsz,) + hist_shape[1:], jnp.float32)]
    return pl.pallas_call(
        functools.partial(_prompt_body, tm),
        grid=(bsz, nt),
        in_specs=in_specs,
        out_specs=out_specs,
        out_shape=out_shape,
        scratch_shapes=_tile_scratch(1, tm),
        compiler_params=pltpu.CompilerParams(
            dimension_semantics=("arbitrary", "arbitrary"),
            vmem_limit_bytes=VMEM_LIMIT_BYTES),
        name="prompt_layer",
    )(x, mod, x, mod, *weights)


def _sample_call(x, mod, state2d, weights):
    bsz, tlen, _ = x.shape
    assert tlen % ROW_CHUNK == 0 and tlen < GMLP_CHUNK and bsz * HEAD_DIM == D
    hist_shape = (bsz, CONV_HIST * COL_BLOCKS, LANES)
    whole = lambda shape: pl.BlockSpec(shape, lambda i: (0,) * len(shape))
    in_specs = [whole(x.shape), whole(mod.shape), whole(hist_shape)]
    in_specs += [_resident(p.shape, 1) for p in weights]
    out_specs = [whole(x.shape), whole(hist_shape), whole(x.shape)]
    out_shape = [jax.ShapeDtypeStruct(x.shape, jnp.float32),
                 jax.ShapeDtypeStruct(hist_shape, jnp.float32),
                 jax.ShapeDtypeStruct(x.shape, jnp.float32)]
    scratch = [pltpu.VMEM((GMLP_CHUNK, bsz * HEAD_DIM), jnp.bfloat16)]
    scratch += _tile_scratch(bsz, tlen)
    return pl.pallas_call(
        functools.partial(_sample_body, bsz, tlen),
        grid=(1,),
        in_specs=in_specs,
        out_specs=out_specs,
        out_shape=out_shape,
        scratch_shapes=scratch,
        compiler_params=pltpu.CompilerParams(
            dimension_semantics=("arbitrary",),
            vmem_limit_bytes=VMEM_LIMIT_BYTES),
        name="sample_layer",
    )(x, mod, state2d, *weights)


def kernel(x_prompt, x_sample, state_conv, c_prompt, c_sample, w_ada, b_ada, norm_g, w_in, b_in, ln_v_g, ln_v_b, w_spatial, b_spatial, conv_w, conv_b, ln_c_g, ln_c_b, w_o_a, w_o_b, w_out, final_g):
    assert w_ada.shape[0] == 1, "single-layer kernel"
    bf16 = jnp.bfloat16
    n_prompt = x_prompt.shape[0]
    n_sample = x_sample.shape[0]

    mod = _adaln(jnp.concatenate([c_prompt, c_sample], axis=0), w_ada[0], b_ada[0][None])
    mod = mod[:, None, :]

    tril = jnp.tril(jnp.ones((GMLP_CHUNK, GMLP_CHUNK), dtype=bool))
    wm = jnp.where(tril[None], w_spatial[0], 0.0).astype(bf16)
    bsp = jnp.repeat(b_spatial[0].T, HEAD_DIM, axis=1)
    row = lambda a: a.reshape(1, -1)
    halved = jnp.zeros((N_IN_BLOCKS,), bool).at[jnp.array(HALVED_BLOCKS)].set(True)
    in_scale = jnp.repeat(jnp.where(halved, 0.5, 1.0), D)[None, :]
    weights = (row(norm_g[0]), (w_in[0] * in_scale).astype(bf16), row(b_in[0]) * in_scale,
               row(ln_v_g[0]), row(ln_v_b[0]), wm, bsp,
               conv_w[0].reshape(CONV_WIDTH * COL_BLOCKS, LANES), row(conv_b[0]), row(ln_c_g[0]), row(ln_c_b[0]),
               w_o_a[0].astype(bf16), w_o_b[0].astype(bf16), w_out[0].astype(bf16), row(final_g))
    assert len(weights) == N_WEIGHTS

    y_prompt, hist_prompt = _prompt_call(x_prompt, mod[:n_prompt], weights, PROMPT_TILE)
    state2d = state_conv[0].reshape(n_sample, CONV_HIST * COL_BLOCKS, LANES)
    y_sample, hist_sample, vn_sample = _sample_call(x_sample, mod[n_prompt:], state2d, weights)

    new_conv_prompt = hist_prompt.reshape(1, n_prompt, CONV_HIST, D)
    new_conv_sample = hist_sample.reshape(1, n_sample, CONV_HIST, D)
    return (y_prompt, y_sample, new_conv_prompt, new_conv_sample, vn_sample[None])
```

```python
import functools

import jax
import jax.numpy as jnp
from jax import lax
from jax.experimental import pallas as pl
from jax.experimental.pallas import tpu as pltpu

D = 1024
N_IN_BLOCKS = 8
GMLP_CHUNK = 128
N_HEADS = 8
HEAD_DIM = D // N_HEADS
CONV_WIDTH = 31
CONV_HIST = CONV_WIDTH - 1
EPS = 1e-6

LANES = 128
SUBLANES = 8
COL_BLOCKS = D // LANES
HIST_PAD = 32
HIST_SKIP = HIST_PAD - CONV_HIST
HIST_ROWS = HIST_PAD * COL_BLOCKS
ROW_CHUNK = 16
CONV_TBLOCK = 4
PROMPT_TILE = 256
VMEM_LIMIT_BYTES = 56 * 1024 * 1024

Z_U, Z_V, Z_GATE_A, Z_GLU_A, Z_GLU_B, Z_GATE_B, Z_MERGE_A, Z_MERGE_B = range(N_IN_BLOCKS)

HALVED_BLOCKS = (Z_GATE_A, Z_GLU_B, Z_GATE_B, Z_MERGE_A, Z_MERGE_B)

_GELU_C = 0.7978845608028654
_GELU_K = 0.044715


def _gelu(x):
    half = 0.5 * x
    return half + half * jnp.tanh(x * (_GELU_C + (_GELU_C * _GELU_K) * (x * x)))


def _gelu_doubled(x):
    return x + x * jnp.tanh(x * (_GELU_C + (_GELU_C * _GELU_K) * (x * x)))


def _sigmoid_of_double(xh):
    return 0.5 * jnp.tanh(xh) + 0.5


def _silu_of_double(xh):
    return xh * jnp.tanh(xh) + xh


def _silu(x):
    return _silu_of_double(0.5 * x)


def _layernorm(x, g, b):
    mu = jnp.mean(x, axis=-1, keepdims=True)
    xc = x - mu
    var = jnp.mean(xc * xc, axis=-1, keepdims=True)
    return xc * lax.rsqrt(var + EPS) * g + b


def _rms_scale(x):
    return lax.rsqrt(jnp.mean(x * x, axis=-1, keepdims=True) + EPS)


def _zero_of(token):
    return (pltpu.bitcast(token, jnp.uint32) >> 16) >> 16


def _adaln_body(c_ref, w_ref, b_ref, o_ref):
    a = _silu(c_ref[...]).astype(jnp.bfloat16)
    o_ref[...] = jnp.dot(a, w_ref[...].astype(jnp.bfloat16),
                         preferred_element_type=jnp.float32) + b_ref[...]


def _adaln(c_all, w_ada, b_ada):
    n = c_all.shape[0]
    blk = 3 * D // 4
    return pl.pallas_call(
        _adaln_body,
        grid=(4,),
        in_specs=[pl.BlockSpec((n, D), lambda j: (0, 0)),
                  pl.BlockSpec((D, blk), lambda j: (0, j)),
                  pl.BlockSpec((1, blk), lambda j: (0, j))],
        out_specs=pl.BlockSpec((n, blk), lambda j: (0, j)),
        out_shape=jax.ShapeDtypeStruct((n, 3 * D), jnp.float32),
        name="adaln",
    )(c_all, w_ada, b_ada)


class _Weights:
    def __init__(self, refs):
        (self.norm_g, self.w_in, self.b_in, self.ln_v_g, self.ln_v_b, self.wm, self.bsp, self.conv_w,
         self.conv_b, self.ln_c_g, self.ln_c_b, self.w_o_a, self.w_o_b, self.w_out, self.final_g) = refs


N_WEIGHTS = 15


class _Scratch:
    def __init__(self, refs):
        (self.hb, self.zglu, self.za, self.zgm, self.vn, self.sp, self.acta, self.actb, self.actm,
         self.ya, self.yb, self.o, self.s2, self.d2) = refs


def _tile_scratch(n_streams, tlen):
    f32, bf16 = jnp.float32, jnp.bfloat16
    tm = n_streams * tlen
    return [
        pltpu.VMEM((tm, D), bf16),
        pltpu.VMEM((tm, 2 * D), f32),
        pltpu.VMEM((tm, 3 * D), f32),
        pltpu.VMEM((tm, 3 * D), f32),
        pltpu.VMEM((tm, D), bf16),
        pltpu.VMEM((tm, D), f32),
        pltpu.VMEM((tm, D), bf16),
        pltpu.VMEM((tm, D), bf16),
        pltpu.VMEM((tm, D), bf16),
        pltpu.VMEM((tm, D), f32),
        pltpu.VMEM((tm, D), f32),
        pltpu.VMEM((tm, D), f32),
        pltpu.VMEM((n_streams * (tlen + HIST_PAD) * COL_BLOCKS, LANES), f32),
        pltpu.VMEM((tm * COL_BLOCKS, LANES), f32),
    ]


def _in_proj(w, sc, dst_ref, blk0, nblk):
    cols = slice(blk0 * D, (blk0 + nblk) * D)
    z = jnp.dot(sc.hb[...], w.w_in[:, cols], preferred_element_type=jnp.float32) + w.b_in[:, cols]
    dst_ref[...] = z
    return z[0:SUBLANES, 0:LANES]


def _tile_head(n_streams, tlen, x_ref, mod_ref, w, sc):
    for s in range(n_streams):
        shift = mod_ref[s, :, 0:D]
        gs = w.norm_g[...] * (1.0 + mod_ref[s, :, D:2 * D])
        for r in range(0, tlen, ROW_CHUNK):
            x = x_ref[s, pl.ds(r, ROW_CHUNK), :]
            h = (x * _rms_scale(x)) * gs + shift
            sc.hb[pl.ds(s * tlen + r, ROW_CHUNK), :] = h.astype(jnp.bfloat16)
    return _in_proj(w, sc, sc.zglu, Z_GLU_A, 2)


def _s2_base(s, tlen):
    return s * (HIST_PAD + tlen) * COL_BLOCKS


def _tile_body(n_streams, tlen, x_ref, mod_ref, next_head, w, sc, y_ref, hist_ref, vn_out_ref, rhs_ref):
    f32, bf16 = jnp.float32, jnp.bfloat16
    tm = n_streams * tlen
    chunks = [(s, r) for s in range(n_streams) for r in range(0, tlen, ROW_CHUNK)]

    for s in range(n_streams):
        rows = pl.ds(s * tlen, tlen)
        for c in range(COL_BLOCKS):
            a = sc.zglu[rows, c * LANES:(c + 1) * LANES]
            b = sc.zglu[rows, D + c * LANES:D + (c + 1) * LANES]
            sc.s2[pl.ds(_s2_base(s, tlen) + HIST_ROWS + c, tlen, stride=COL_BLOCKS), :] = a * _sigmoid_of_double(b)

    _in_proj(w, sc, sc.za, Z_U, 3)
    _in_proj(w, sc, sc.zgm, Z_GATE_B, 3)

    head_done = next_head() if next_head is not None else None

    tb = min(CONV_TBLOCK, tlen)
    token = None
    for s in range(n_streams):
        base = _s2_base(s, tlen)
        for t0 in range(0, tlen, tb):
            w0 = w.conv_w[0:COL_BLOCKS, :]
            if token is not None:
                w0 = w0 + pltpu.bitcast(_zero_of(token), f32)
            acc = sc.s2[pl.ds(base + (t0 + HIST_SKIP) * COL_BLOCKS, tb * COL_BLOCKS), :] * jnp.tile(w0, (tb, 1))
            for k in range(1, CONV_WIDTH):
                wk = jnp.tile(w.conv_w[k * COL_BLOCKS:(k + 1) * COL_BLOCKS, :], (tb, 1))
                acc = acc + sc.s2[pl.ds(base + (t0 + k + HIST_SKIP) * COL_BLOCKS, tb * COL_BLOCKS), :] * wk
            sc.d2[pl.ds((s * tlen + t0) * COL_BLOCKS, tb * COL_BLOCKS), :] = acc
            token = acc[0:COL_BLOCKS, :]
            for i in range(1, tb):
                token = token + acc[i * COL_BLOCKS:(i + 1) * COL_BLOCKS, :]
        hist_ref[s] = sc.s2[pl.ds(base + (tlen + HIST_SKIP) * COL_BLOCKS, CONV_HIST * COL_BLOCKS), :]
        sc.s2[pl.ds(base, HIST_ROWS), :] = sc.s2[pl.ds(base + tlen * COL_BLOCKS, HIST_ROWS), :]

    ln_v_g = w.ln_v_g[...]
    ln_v_b = w.ln_v_b[...]
    for s, r in chunks:
        rows = pl.ds(s * tlen + r, ROW_CHUNK)
        vn = _layernorm(_gelu(sc.za[rows, D:2 * D]), ln_v_g, ln_v_b)
        if vn_out_ref is not None:
            vn_out_ref[s, pl.ds(r, ROW_CHUNK), :] = vn
        sc.vn[rows, :] = vn.astype(bf16)

    if tlen % GMLP_CHUNK == 0:
        n_chunks = tm // GMLP_CHUNK
        for hd in range(N_HEADS):
            cols = slice(hd * HEAD_DIM, (hd + 1) * HEAD_DIM)
            rhs = jnp.concatenate(
                [sc.vn[n * GMLP_CHUNK:(n + 1) * GMLP_CHUNK, cols] for n in range(n_chunks)], axis=1)
            sp = jnp.dot(w.wm[hd], rhs, preferred_element_type=f32)
            for n in range(n_chunks):
                sc.sp[n * GMLP_CHUNK:(n + 1) * GMLP_CHUNK, cols] = (
                    sp[:, n * HEAD_DIM:(n + 1) * HEAD_DIM] + w.bsp[:, cols])
    else:
        rhs_ref[...] = jnp.zeros(rhs_ref.shape, bf16)
        for hd in range(N_HEADS):
            cols = slice(hd * HEAD_DIM, (hd + 1) * HEAD_DIM)
            for s in range(n_streams):
                rhs_ref[0:tlen, s * HEAD_DIM:(s + 1) * HEAD_DIM] = sc.vn[pl.ds(s * tlen, tlen), cols]
            sp = jnp.dot(w.wm[hd, 0:tlen, :], rhs_ref[...], preferred_element_type=f32)
            for s in range(n_streams):
                sc.sp[pl.ds(s * tlen, tlen), cols] = sp[:, s * HEAD_DIM:(s + 1) * HEAD_DIM] + w.bsp[0:tlen, cols]

    conv_b = w.conv_b[...]
    if head_done is not None:
        zero_row = pltpu.bitcast(_zero_of(head_done), f32)[0:1, :]
        conv_b = conv_b + jnp.tile(zero_row, (1, COL_BLOCKS))
    ln_c_g = w.ln_c_g[...]
    ln_c_b = w.ln_c_b[...]
    for s, r in chunks:
        row0 = s * tlen + r
        dw = jnp.concatenate(
            [sc.d2[pl.ds(row0 * COL_BLOCKS + c, ROW_CHUNK, stride=COL_BLOCKS), :] for c in range(COL_BLOCKS)],
            axis=1) + conv_b
        yb = _silu_of_double(_layernorm(dw, ln_c_g, ln_c_b)) * _silu_of_double(sc.zgm[pl.ds(row0, ROW_CHUNK), 0:D])
        sc.actb[pl.ds(row0, ROW_CHUNK), :] = yb.astype(bf16)
    sc.yb[...] = jnp.dot(sc.actb[...], w.w_o_b[...], preferred_element_type=f32)

    for r in range(0, tm, ROW_CHUNK):
        rows = pl.ds(r, ROW_CHUNK)
        ya = _gelu_doubled(sc.za[rows, 0:D]) * sc.sp[rows, :] * _silu_of_double(sc.za[rows, 2 * D:3 * D])
        sc.acta[rows, :] = ya.astype(bf16)
    sc.ya[...] = jnp.dot(sc.acta[...], w.w_o_a[...], preferred_element_type=f32)

    for r in range(0, tm, ROW_CHUNK):
        rows = pl.ds(r, ROW_CHUNK)
        m = ((jnp.tanh(sc.zgm[rows, D:2 * D]) + 1.0) * sc.ya[rows, :]
             + (jnp.tanh(sc.zgm[rows, 2 * D:3 * D]) + 1.0) * sc.yb[rows, :])
        sc.actm[rows, :] = m.astype(bf16)
    sc.o[...] = jnp.dot(sc.actm[...], w.w_out[...], preferred_element_type=f32)
    final_g = w.final_g[...]
    for s, r in chunks:
        gate = mod_ref[s, :, 2 * D:3 * D]
        xo = x_ref[s, pl.ds(r, ROW_CHUNK), :] + gate * sc.o[pl.ds(s * tlen + r, ROW_CHUNK), :]
        y_ref[s, pl.ds(r, ROW_CHUNK), :] = (xo * _rms_scale(xo)) * final_g


def _prompt_body(tm, *refs):
    x_ref, mod_ref, xn_ref, modn_ref = refs[0:4]
    w = _Weights(refs[4:4 + N_WEIGHTS])
    y_ref, hist_ref = refs[4 + N_WEIGHTS:6 + N_WEIGHTS]
    sc = _Scratch(refs[6 + N_WEIGHTS:])
    b = pl.program_id(0)
    t = pl.program_id(1)

    @pl.when(jnp.logical_and(b == 0, t == 0))
    def _():
        _tile_head(1, tm, x_ref, mod_ref, w, sc)

    @pl.when(t == 0)
    def _():
        sc.s2[pl.ds(0, HIST_ROWS), :] = jnp.zeros((HIST_ROWS, LANES), jnp.float32)

    next_head = functools.partial(_tile_head, 1, tm, xn_ref, modn_ref, w, sc)
    _tile_body(1, tm, x_ref, mod_ref, next_head, w, sc, y_ref, hist_ref, None, None)


def _sample_body(n_streams, tlen, *refs):
    x_ref, mod_ref, state_ref = refs[0:3]
    w = _Weights(refs[3:3 + N_WEIGHTS])
    y_ref, hist_ref, vn_out_ref, rhs_ref = refs[3 + N_WEIGHTS:7 + N_WEIGHTS]
    sc = _Scratch(refs[7 + N_WEIGHTS:])
    for s in range(n_streams):
        base = _s2_base(s, tlen)
        sc.s2[pl.ds(base, HIST_SKIP * COL_BLOCKS), :] = jnp.zeros((HIST_SKIP * COL_BLOCKS, LANES), jnp.float32)
        sc.s2[pl.ds(base + HIST_SKIP * COL_BLOCKS, CONV_HIST * COL_BLOCKS), :] = state_ref[s]
    _tile_head(n_streams, tlen, x_ref, mod_ref, w, sc)
    _tile_body(n_streams, tlen, x_ref, mod_ref, None, w, sc, y_ref, hist_ref, vn_out_ref, rhs_ref)


def _resident(shape, n_grid_axes):
    zeros = (0,) * len(shape)
    if n_grid_axes == 1:
        return pl.BlockSpec(shape, lambda b: zeros, pipeline_mode=pl.Buffered(1))
    return pl.BlockSpec(shape, lambda b, t: zeros, pipeline_mode=pl.Buffered(1))


def _prompt_call(x, mod, weights, tm):
    bsz, tlen, _ = x.shape
    assert tlen % tm == 0 and tm % GMLP_CHUNK == 0
    nt = tlen // tm
    n_tiles = bsz * nt
    hist_shape = (1, CONV_HIST * COL_BLOCKS, LANES)

    def following(b, t):
        j = jnp.minimum(b * nt + t + 1, n_tiles - 1)
        return j // nt, j % nt

    in_specs = [pl.BlockSpec((1, tm, D), lambda b, t: (b, t, 0)),
                pl.BlockSpec((1, 1, 3 * D), lambda b, t: (b, 0, 0)),
                pl.BlockSpec((1, tm, D), lambda b, t: (*following(b, t), 0)),
                pl.BlockSpec((1, 1, 3 * D), lambda b, t: (following(b, t)[0], 0, 0))]
    in_specs += [_resident(p.shape, 2) for p in weights]
    out_specs = [pl.BlockSpec((1, tm, D), lambda b, t: (b, t, 0)),
                 pl.BlockSpec(hist_shape, lambda b, t: (b, 0, 0))]
    out_shape = [jax.ShapeDtypeStruct((bsz, tlen, D), jnp.float32),
                 jax.ShapeDtypeStruct((bsz,) + hist_shape[1:], jnp.float32)]
    return pl.pallas_call(
        functools.partial(_prompt_body, tm),
        grid=(bsz, nt),
        in_specs=in_specs,
        out_specs=out_specs,
        out_shape=out_shape,
        scratch_shapes=_tile_scratch(1, tm),
        compiler_params=pltpu.CompilerParams(
            dimension_semantics=("arbitrary", "arbitrary"),
            vmem_limit_bytes=VMEM_LIMIT_BYTES),
        name="prompt_layer",
    )(x, mod, x, mod, *weights)


def _sample_call(x, mod, state2d, weights):
    bsz, tlen, _ = x.shape
    assert tlen % ROW_CHUNK == 0 and tlen < GMLP_CHUNK and bsz * HEAD_DIM == D
    hist_shape = (bsz, CONV_HIST * COL_BLOCKS, LANES)
    whole = lambda shape: pl.BlockSpec(shape, lambda i: (0,) * len(shape))
    in_specs = [whole(x.shape), whole(mod.shape), whole(hist_shape)]
    in_specs += [_resident(p.shape, 1) for p in weights]
    out_specs = [whole(x.shape), whole(hist_shape), whole(x.shape)]
    out_shape = [jax.ShapeDtypeStruct(x.shape, jnp.float32),
                 jax.ShapeDtypeStruct(hist_shape, jnp.float32),
                 jax.ShapeDtypeStruct(x.shape, jnp.float32)]
    scratch = [pltpu.VMEM((GMLP_CHUNK, bsz * HEAD_DIM), jnp.bfloat16)]
    scratch += _tile_scratch(bsz, tlen)
    return pl.pallas_call(
        functools.partial(_sample_body, bsz, tlen),
        grid=(1,),
        in_specs=in_specs,
        out_specs=out_specs,
        out_shape=out_shape,
        scratch_shapes=scratch,
        compiler_params=pltpu.CompilerParams(
            dimension_semantics=("arbitrary",),
            vmem_limit_bytes=VMEM_LIMIT_BYTES),
        name="sample_layer",
    )(x, mod, state2d, *weights)


def kernel(x_prompt, x_sample, state_conv, c_prompt, c_sample, w_ada, b_ada, norm_g, w_in, b_in, ln_v_g, ln_v_b, w_spatial, b_spatial, conv_w, conv_b, ln_c_g, ln_c_b, w_o_a, w_o_b, w_out, final_g):
    assert w_ada.shape[0] == 1, "single-layer kernel"
    bf16 = jnp.bfloat16
    n_prompt = x_prompt.shape[0]
    n_sample = x_sample.shape[0]

    mod = _adaln(jnp.concatenate([c_prompt, c_sample], axis=0), w_ada[0], b_ada[0][None])
    mod = mod[:, None, :]

    tril = jnp.tril(jnp.ones((GMLP_CHUNK, GMLP_CHUNK), dtype=bool))
    wm = jnp.where(tril[None], 0.5 * w_spatial[0], 0.0).astype(bf16)
    bsp = jnp.repeat(0.5 * b_spatial[0].T, HEAD_DIM, axis=1)
    row = lambda a: a.reshape(1, -1)
    halved = jnp.zeros((N_IN_BLOCKS,), bool).at[jnp.array(HALVED_BLOCKS)].set(True)
    in_scale = jnp.repeat(jnp.where(halved, 0.5, 1.0), D)[None, :]
    weights = (row(norm_g[0]), (w_in[0] * in_scale).astype(bf16), row(b_in[0]) * in_scale,
               row(ln_v_g[0]), row(ln_v_b[0]), wm, bsp,
               conv_w[0].reshape(CONV_WIDTH * COL_BLOCKS, LANES), row(conv_b[0]), row(0.5 * ln_c_g[0]), row(0.5 * ln_c_b[0]),
               w_o_a[0].astype(bf16), w_o_b[0].astype(bf16), (0.5 * w_out[0]).astype(bf16), row(final_g))
    assert len(weights) == N_WEIGHTS

    y_prompt, hist_prompt = _prompt_call(x_prompt, mod[:n_prompt], weights, PROMPT_TILE)
    state2d = state_conv[0].reshape(n_sample, CONV_HIST * COL_BLOCKS, LANES)
    y_sample, hist_sample, vn_sample = _sample_call(x_sample, mod[n_prompt:], state2d, weights)

    new_conv_prompt = hist_prompt.reshape(1, n_prompt, CONV_HIST, D)
    new_conv_sample = hist_sample.reshape(1, n_sample, CONV_HIST, D)
    return (y_prompt, y_sample, new_conv_prompt, new_conv_sample, vn_sample[None])
```

```python
import functools

import jax
import jax.numpy as jnp
from jax import lax
from jax.experimental import pallas as pl
from jax.experimental.pallas import tpu as pltpu

D = 1024
N_IN_BLOCKS = 8
GMLP_CHUNK = 128
N_HEADS = 8
HEAD_DIM = D // N_HEADS
CONV_WIDTH = 31
CONV_HIST = CONV_WIDTH - 1
EPS = 1e-6

LANES = 128
SUBLANES = 8
COL_BLOCKS = D // LANES
HIST_PAD = 32
HIST_SKIP = HIST_PAD - CONV_HIST
HIST_ROWS = HIST_PAD * COL_BLOCKS
ROW_CHUNK = 16
CONV_TBLOCK = 4
PROMPT_TILE = 256
VMEM_LIMIT_BYTES = 56 * 1024 * 1024

Z_U, Z_V, Z_GATE_A, Z_GLU_A, Z_GLU_B, Z_GATE_B, Z_MERGE_A, Z_MERGE_B = range(N_IN_BLOCKS)

HALVED_BLOCKS = (Z_V, Z_GATE_A, Z_GLU_B, Z_GATE_B, Z_MERGE_A, Z_MERGE_B)

_GELU_C = 0.7978845608028654
_GELU_K = 0.044715


def _gelu_of_double(xh):
    return xh + xh * jnp.tanh(xh * (2.0 * _GELU_C + (8.0 * _GELU_C * _GELU_K) * (xh * xh)))


def _gelu_doubled(x):
    return x + x * jnp.tanh(x * (_GELU_C + (_GELU_C * _GELU_K) * (x * x)))


def _sigmoid_of_double(xh):
    return 0.5 * jnp.tanh(xh) + 0.5


def _silu_of_double(xh):
    return xh * jnp.tanh(xh) + xh


def _silu(x):
    return _silu_of_double(0.5 * x)


def _layernorm(x, g, b):
    mu = jnp.mean(x, axis=-1, keepdims=True)
    xc = x - mu
    var = jnp.mean(xc * xc, axis=-1, keepdims=True)
    return xc * lax.rsqrt(var + EPS) * g + b


def _rms_scale(x):
    return lax.rsqrt(jnp.mean(x * x, axis=-1, keepdims=True) + EPS)


def _zero_of(token):
    return (pltpu.bitcast(token, jnp.uint32) >> 16) >> 16


def _adaln_body(c_ref, w_ref, b_ref, o_ref):
    a = _silu(c_ref[...]).astype(jnp.bfloat16)
    o_ref[...] = jnp.dot(a, w_ref[...].astype(jnp.bfloat16),
                         preferred_element_type=jnp.float32) + b_ref[...]


def _adaln(c_all, w_ada, b_ada):
    n = c_all.shape[0]
    blk = 3 * D // 4
    return pl.pallas_call(
        _adaln_body,
        grid=(4,),
        in_specs=[pl.BlockSpec((n, D), lambda j: (0, 0)),
                  pl.BlockSpec((D, blk), lambda j: (0, j)),
                  pl.BlockSpec((1, blk), lambda j: (0, j))],
        out_specs=pl.BlockSpec((n, blk), lambda j: (0, j)),
        out_shape=jax.ShapeDtypeStruct((n, 3 * D), jnp.float32),
        name="adaln",
    )(c_all, w_ada, b_ada)


class _Weights:
    def __init__(self, refs):
        (self.norm_g, self.w_in, self.b_in, self.ln_v_g, self.ln_v_b, self.wm, self.bsp, self.conv_w,
         self.conv_b, self.ln_c_g, self.ln_c_b, self.w_o_a, self.w_o_b, self.w_out, self.final_g) = refs


N_WEIGHTS = 15


class _Scratch:
    def __init__(self, refs):
        (self.hb, self.zglu, self.za, self.zgm, self.vn, self.sp, self.acta, self.actb, self.actm,
         self.ya, self.yb, self.o, self.s2, self.d2) = refs


def _tile_scratch(n_streams, tlen):
    f32, bf16 = jnp.float32, jnp.bfloat16
    tm = n_streams * tlen
    return [
        pltpu.VMEM((tm, D), bf16),
        pltpu.VMEM((tm, 2 * D), f32),
        pltpu.VMEM((tm, 3 * D), f32),
        pltpu.VMEM((tm, 3 * D), f32),
        pltpu.VMEM((tm, D), bf16),
        pltpu.VMEM((tm, D), f32),
        pltpu.VMEM((tm, D), bf16),
        pltpu.VMEM((tm, D), bf16),
        pltpu.VMEM((tm, D), bf16),
        pltpu.VMEM((tm, D), f32),
        pltpu.VMEM((tm, D), f32),
        pltpu.VMEM((tm, D), f32),
        pltpu.VMEM((n_streams * (tlen + HIST_PAD) * COL_BLOCKS, LANES), f32),
        pltpu.VMEM((tm * COL_BLOCKS, LANES), f32),
    ]


def _in_proj(w, sc, dst_ref, blk0, nblk):
    cols = slice(blk0 * D, (blk0 + nblk) * D)
    z = jnp.dot(sc.hb[...], w.w_in[:, cols], preferred_element_type=jnp.float32) + w.b_in[:, cols]
    dst_ref[...] = z
    return z[0:SUBLANES, 0:LANES]


def _tile_head(n_streams, tlen, x_ref, mod_ref, w, sc):
    for s in range(n_streams):
        shift = mod_ref[s, :, 0:D]
        gs = w.norm_g[...] * (1.0 + mod_ref[s, :, D:2 * D])
        for r in range(0, tlen, ROW_CHUNK):
            x = x_ref[s, pl.ds(r, ROW_CHUNK), :]
            h = (x * _rms_scale(x)) * gs + shift
            sc.hb[pl.ds(s * tlen + r, ROW_CHUNK), :] = h.astype(jnp.bfloat16)
    return _in_proj(w, sc, sc.zglu, Z_GLU_A, 2)


def _s2_base(s, tlen):
    return s * (HIST_PAD + tlen) * COL_BLOCKS


def _tile_body(n_streams, tlen, x_ref, mod_ref, next_head, w, sc, y_ref, hist_ref, vn_out_ref, rhs_ref):
    f32, bf16 = jnp.float32, jnp.bfloat16
    tm = n_streams * tlen
    chunks = [(s, r) for s in range(n_streams) for r in range(0, tlen, ROW_CHUNK)]

    for s in range(n_streams):
        rows = pl.ds(s * tlen, tlen)
        for c in range(COL_BLOCKS):
            a = sc.zglu[rows, c * LANES:(c + 1) * LANES]
            b = sc.zglu[rows, D + c * LANES:D + (c + 1) * LANES]
            sc.s2[pl.ds(_s2_base(s, tlen) + HIST_ROWS + c, tlen, stride=COL_BLOCKS), :] = a * _sigmoid_of_double(b)

    _in_proj(w, sc, sc.za, Z_U, 3)
    _in_proj(w, sc, sc.zgm, Z_GATE_B, 3)

    head_done = next_head() if next_head is not None else None

    tb = min(CONV_TBLOCK, tlen)
    token = None
    for s in range(n_streams):
        base = _s2_base(s, tlen)
        for t0 in range(0, tlen, tb):
            w0 = w.conv_w[0:COL_BLOCKS, :]
            if token is not None:
                w0 = w0 + pltpu.bitcast(_zero_of(token), f32)
            acc = sc.s2[pl.ds(base + (t0 + HIST_SKIP) * COL_BLOCKS, tb * COL_BLOCKS), :] * jnp.tile(w0, (tb, 1))
            for k in range(1, CONV_WIDTH):
                wk = jnp.tile(w.conv_w[k * COL_BLOCKS:(k + 1) * COL_BLOCKS, :], (tb, 1))
                acc = acc + sc.s2[pl.ds(base + (t0 + k + HIST_SKIP) * COL_BLOCKS, tb * COL_BLOCKS), :] * wk
            sc.d2[pl.ds((s * tlen + t0) * COL_BLOCKS, tb * COL_BLOCKS), :] = acc
            token = acc[0:COL_BLOCKS, :]
            for i in range(1, tb):
                token = token + acc[i * COL_BLOCKS:(i + 1) * COL_BLOCKS, :]
        hist_ref[s] = sc.s2[pl.ds(base + (tlen + HIST_SKIP) * COL_BLOCKS, CONV_HIST * COL_BLOCKS), :]
        sc.s2[pl.ds(base, HIST_ROWS), :] = sc.s2[pl.ds(base + tlen * COL_BLOCKS, HIST_ROWS), :]

    ln_v_g = w.ln_v_g[...]
    ln_v_b = w.ln_v_b[...]
    for s, r in chunks:
        rows = pl.ds(s * tlen + r, ROW_CHUNK)
        vn = _layernorm(_gelu_of_double(sc.za[rows, D:2 * D]), ln_v_g, ln_v_b)
        if vn_out_ref is not None:
            vn_out_ref[s, pl.ds(r, ROW_CHUNK), :] = vn
        sc.vn[rows, :] = vn.astype(bf16)

    if tlen % GMLP_CHUNK == 0:
        n_chunks = tm // GMLP_CHUNK
        for hd in range(N_HEADS):
            cols = slice(hd * HEAD_DIM, (hd + 1) * HEAD_DIM)
            rhs = jnp.concatenate(
                [sc.vn[n * GMLP_CHUNK:(n + 1) * GMLP_CHUNK, cols] for n in range(n_chunks)], axis=1)
            sp = jnp.dot(w.wm[hd], rhs, preferred_element_type=f32)
            for n in range(n_chunks):
                sc.sp[n * GMLP_CHUNK:(n + 1) * GMLP_CHUNK, cols] = (
                    sp[:, n * HEAD_DIM:(n + 1) * HEAD_DIM] + w.bsp[:, cols])
    else:
        rhs_ref[...] = jnp.zeros(rhs_ref.shape, bf16)
        for hd in range(N_HEADS):
            cols = slice(hd * HEAD_DIM, (hd + 1) * HEAD_DIM)
            for s in range(n_streams):
                rhs_ref[0:tlen, s * HEAD_DIM:(s + 1) * HEAD_DIM] = sc.vn[pl.ds(s * tlen, tlen), cols]
            sp = jnp.dot(w.wm[hd, 0:tlen, :], rhs_ref[...], preferred_element_type=f32)
            for s in range(n_streams):
                sc.sp[pl.ds(s * tlen, tlen), cols] = sp[:, s * HEAD_DIM:(s + 1) * HEAD_DIM] + w.bsp[0:tlen, cols]

    conv_b = w.conv_b[...]
    if head_done is not None:
        zero_row = pltpu.bitcast(_zero_of(head_done), f32)[0:1, :]
        conv_b = conv_b + jnp.tile(zero_row, (1, COL_BLOCKS))
    ln_c_g = w.ln_c_g[...]
    ln_c_b = w.ln_c_b[...]
    for s, r in chunks:
        row0 = s * tlen + r
        dw = jnp.concatenate(
            [sc.d2[pl.ds(row0 * COL_BLOCKS + c, ROW_CHUNK, stride=COL_BLOCKS), :] for c in range(COL_BLOCKS)],
            axis=1) + conv_b
        yb = _silu_of_double(_layernorm(dw, ln_c_g, ln_c_b)) * _silu_of_double(sc.zgm[pl.ds(row0, ROW_CHUNK), 0:D])
        sc.actb[pl.ds(row0, ROW_CHUNK), :] = yb.astype(bf16)
    sc.yb[...] = jnp.dot(sc.actb[...], w.w_o_b[...], preferred_element_type=f32)

    for r in range(0, tm, ROW_CHUNK):
        rows = pl.ds(r, ROW_CHUNK)
        ya = _gelu_doubled(sc.za[rows, 0:D]) * sc.sp[rows, :] * _silu_of_double(sc.za[rows, 2 * D:3 * D])
        sc.acta[rows, :] = ya.astype(bf16)
    sc.ya[...] = jnp.dot(sc.acta[...], w.w_o_a[...], preferred_element_type=f32)

    for r in range(0, tm, ROW_CHUNK):
        rows = pl.ds(r, ROW_CHUNK)
        m = ((jnp.tanh(sc.zgm[rows, D:2 * D]) + 1.0) * sc.ya[rows, :]
             + (jnp.tanh(sc.zgm[rows, 2 * D:3 * D]) + 1.0) * sc.yb[rows, :])
        sc.actm[rows, :] = m.astype(bf16)
    sc.o[...] = jnp.dot(sc.actm[...], w.w_out[...], preferred_element_type=f32)
    final_g = w.final_g[...]
    for s, r in chunks:
        gate = mod_ref[s, :, 2 * D:3 * D]
        xo = x_ref[s, pl.ds(r, ROW_CHUNK), :] + gate * sc.o[pl.ds(s * tlen + r, ROW_CHUNK), :]
        y_ref[s, pl.ds(r, ROW_CHUNK), :] = (xo * _rms_scale(xo)) * final_g


def _prompt_body(tm, *refs):
    x_ref, mod_ref, xn_ref, modn_ref = refs[0:4]
    w = _Weights(refs[4:4 + N_WEIGHTS])
    y_ref, hist_ref = refs[4 + N_WEIGHTS:6 + N_WEIGHTS]
    sc = _Scratch(refs[6 + N_WEIGHTS:])
    b = pl.program_id(0)
    t = pl.program_id(1)

    @pl.when(jnp.logical_and(b == 0, t == 0))
    def _():
        _tile_head(1, tm, x_ref, mod_ref, w, sc)

    @pl.when(t == 0)
    def _():
        sc.s2[pl.ds(0, HIST_ROWS), :] = jnp.zeros((HIST_ROWS, LANES), jnp.float32)

    next_head = functools.partial(_tile_head, 1, tm, xn_ref, modn_ref, w, sc)
    _tile_body(1, tm, x_ref, mod_ref, next_head, w, sc, y_ref, hist_ref, None, None)


def _sample_body(n_streams, tlen, *refs):
    x_ref, mod_ref, state_ref = refs[0:3]
    w = _Weights(refs[3:3 + N_WEIGHTS])
    y_ref, hist_ref, vn_out_ref, rhs_ref = refs[3 + N_WEIGHTS:7 + N_WEIGHTS]
    sc = _Scratch(refs[7 + N_WEIGHTS:])
    for s in range(n_streams):
        base = _s2_base(s, tlen)
        sc.s2[pl.ds(base, HIST_SKIP * COL_BLOCKS), :] = jnp.zeros((HIST_SKIP * COL_BLOCKS, LANES), jnp.float32)
        sc.s2[pl.ds(base + HIST_SKIP * COL_BLOCKS, CONV_HIST * COL_BLOCKS), :] = state_ref[s]
    _tile_head(n_streams, tlen, x_ref, mod_ref, w, sc)
    _tile_body(n_streams, tlen, x_ref, mod_ref, None, w, sc, y_ref, hist_ref, vn_out_ref, rhs_ref)


def _resident(shape, n_grid_axes):
    zeros = (0,) * len(shape)
    if n_grid_axes == 1:
        return pl.BlockSpec(shape, lambda b: zeros, pipeline_mode=pl.Buffered(1))
    return pl.BlockSpec(shape, lambda b, t: zeros, pipeline_mode=pl.Buffered(1))


def _prompt_call(x, mod, weights, tm):
    bsz, tlen, _ = x.shape
    assert tlen % tm == 0 and tm % GMLP_CHUNK == 0
    nt = tlen // tm
    n_tiles = bsz * nt
    hist_shape = (1, CONV_HIST * COL_BLOCKS, LANES)

    def following(b, t):
        j = jnp.minimum(b * nt + t + 1, n_tiles - 1)
        return j // nt, j % nt

    in_specs = [pl.BlockSpec((1, tm, D), lambda b, t: (b, t, 0)),
                pl.BlockSpec((1, 1, 3 * D), lambda b, t: (b, 0, 0)),
                pl.BlockSpec((1, tm, D), lambda b, t: (*following(b, t), 0)),
                pl.BlockSpec((1, 1, 3 * D), lambda b, t: (following(b, t)[0], 0, 0))]
    in_specs += [_resident(p.shape, 2) for p in weights]
    out_specs = [pl.BlockSpec((1, tm, D), lambda b, t: (b, t, 0)),
                 pl.BlockSpec(hist_shape, lambda b, t: (b, 0, 0))]
    out_shape = [jax.ShapeDtypeStruct((bsz, tlen, D), jnp.float32),
                 jax.ShapeDtypeStruct((bsz,) + hist_shape[1:], jnp.float32)]
    return pl.pallas_call(
        functools.partial(_prompt_body, tm),
        grid=(bsz, nt),
        in_specs=in_specs,
        out_specs=out_specs,
        out_shape=out_shape,
        scratch_shapes=_tile_scratch(1, tm),
        compiler_params=pltpu.CompilerParams(
            dimension_semantics=("arbitrary", "arbitrary"),
            vmem_limit_bytes=VMEM_LIMIT_BYTES),
        name="prompt_layer",
    )(x, mod, x, mod, *weights)


def _sample_call(x, mod, state2d, weights):
    bsz, tlen, _ = x.shape
    assert tlen % ROW_CHUNK == 0 and tlen < GMLP_CHUNK and bsz * HEAD_DIM == D
    hist_shape = (bsz, CONV_HIST * COL_BLOCKS, LANES)
    whole = lambda shape: pl.BlockSpec(shape, lambda i: (0,) * len(shape))
    in_specs = [whole(x.shape), whole(mod.shape), whole(hist_shape)]
    in_specs += [_resident(p.shape, 1) for p in weights]
    out_specs = [whole(x.shape), whole(hist_shape), whole(x.shape)]
    out_shape = [jax.ShapeDtypeStruct(x.shape, jnp.float32),
                 jax.ShapeDtypeStruct(hist_shape, jnp.float32),
                 jax.ShapeDtypeStruct(x.shape, jnp.float32)]
    scratch = [pltpu.VMEM((GMLP_CHUNK, bsz * HEAD_DIM), jnp.bfloat16)]
    scratch += _tile_scratch(bsz, tlen)
    return pl.pallas_call(
        functools.partial(_sample_body, bsz, tlen),
        grid=(1,),
        in_specs=in_specs,
        out_specs=out_specs,
        out_shape=out_shape,
        scratch_shapes=scratch,
        compiler_params=pltpu.CompilerParams(
            dimension_semantics=("arbitrary",),
            vmem_limit_bytes=VMEM_LIMIT_BYTES),
        name="sample_layer",
    )(x, mod, state2d, *weights)


def kernel(x_prompt, x_sample, state_conv, c_prompt, c_sample, w_ada, b_ada, norm_g, w_in, b_in, ln_v_g, ln_v_b, w_spatial, b_spatial, conv_w, conv_b, ln_c_g, ln_c_b, w_o_a, w_o_b, w_out, final_g):
    assert w_ada.shape[0] == 1, "single-layer kernel"
    bf16 = jnp.bfloat16
    n_prompt = x_prompt.shape[0]
    n_sample = x_sample.shape[0]

    mod = _adaln(jnp.concatenate([c_prompt, c_sample], axis=0), w_ada[0], b_ada[0][None])
    mod = mod[:, None, :]

    tril = jnp.tril(jnp.ones((GMLP_CHUNK, GMLP_CHUNK), dtype=bool))
    wm = jnp.where(tril[None], 0.5 * w_spatial[0], 0.0).astype(bf16)
    bsp = jnp.repeat(0.5 * b_spatial[0].T, HEAD_DIM, axis=1)
    row = lambda a: a.reshape(1, -1)
    halved = jnp.zeros((N_IN_BLOCKS,), bool).at[jnp.array(HALVED_BLOCKS)].set(True)
    in_scale = jnp.repeat(jnp.where(halved, 0.5, 1.0), D)[None, :]
    weights = (row(norm_g[0]), (w_in[0] * in_scale).astype(bf16), row(b_in[0]) * in_scale,
               row(ln_v_g[0]), row(ln_v_b[0]), wm, bsp,
               conv_w[0].reshape(CONV_WIDTH * COL_BLOCKS, LANES), row(conv_b[0]), row(0.5 * ln_c_g[0]), row(0.5 * ln_c_b[0]),
               w_o_a[0].astype(bf16), w_o_b[0].astype(bf16), (0.5 * w_out[0]).astype(bf16), row(final_g))
    assert len(weights) == N_WEIGHTS

    y_prompt, hist_prompt = _prompt_call(x_prompt, mod[:n_prompt], weights, PROMPT_TILE)
    state2d = state_conv[0].reshape(n_sample, CONV_HIST * COL_BLOCKS, LANES)
    y_sample, hist_sample, vn_sample = _sample_call(x_sample, mod[n_prompt:], state2d, weights)

    new_conv_prompt = hist_prompt.reshape(1, n_prompt, CONV_HIST, D)
    new_conv_sample = hist_sample.reshape(1, n_sample, CONV_HIST, D)
    return (y_prompt, y_sample, new_conv_prompt, new_conv_sample, vn_sample[None])
```

```python
import functools

import jax
import jax.numpy as jnp
from jax import lax
from jax.experimental import pallas as pl
from jax.experimental.pallas import tpu as pltpu

D = 1024
N_IN_BLOCKS = 8
GMLP_CHUNK = 128
N_HEADS = 8
HEAD_DIM = D // N_HEADS
CONV_WIDTH = 31
CONV_HIST = CONV_WIDTH - 1
EPS = 1e-6

LANES = 128
SUBLANES = 8
COL_BLOCKS = D // LANES
HIST_PAD = 32
HIST_SKIP = HIST_PAD - CONV_HIST
HIST_ROWS = HIST_PAD * COL_BLOCKS
ROW_CHUNK = 16
CONV_TBLOCK = 4
PROMPT_TILE = 256
VMEM_LIMIT_BYTES = 56 * 1024 * 1024

Z_U, Z_V, Z_GATE_A, Z_GLU_A, Z_GLU_B, Z_GATE_B, Z_MERGE_A, Z_MERGE_B = range(N_IN_BLOCKS)

HALVED_BLOCKS = (Z_V, Z_GATE_A, Z_GLU_B, Z_GATE_B, Z_MERGE_A, Z_MERGE_B)

_GELU_C = 0.7978845608028654
_GELU_K = 0.044715


def _gelu_of_double(xh):
    return xh + xh * jnp.tanh(xh * (2.0 * _GELU_C + (8.0 * _GELU_C * _GELU_K) * (xh * xh)))


def _gelu_doubled(x):
    return x + x * jnp.tanh(x * (_GELU_C + (_GELU_C * _GELU_K) * (x * x)))


def _sigmoid_of_double(xh):
    return 0.5 * jnp.tanh(xh) + 0.5


def _silu_of_double(xh):
    return xh * jnp.tanh(xh) + xh


def _silu(x):
    return _silu_of_double(0.5 * x)


def _layernorm(x, g, b):
    mu = jnp.mean(x, axis=-1, keepdims=True)
    xc = x - mu
    var = jnp.mean(xc * xc, axis=-1, keepdims=True)
    return xc * lax.rsqrt(var + EPS) * g + b


def _rms_scale(x):
    return lax.rsqrt(jnp.mean(x * x, axis=-1, keepdims=True) + EPS)


def _zero_of(token):
    return (pltpu.bitcast(token, jnp.uint32) >> 16) >> 16


def _adaln_body(c_ref, w_ref, b_ref, o_ref):
    a = _silu(c_ref[...]).astype(jnp.bfloat16)
    o_ref[...] = jnp.dot(a, w_ref[...].astype(jnp.bfloat16),
                         preferred_element_type=jnp.float32) + b_ref[...]


def _adaln(c_all, w_ada, b_ada):
    n = c_all.shape[0]
    blk = 3 * D // 4
    return pl.pallas_call(
        _adaln_body,
        grid=(4,),
        in_specs=[pl.BlockSpec((n, D), lambda j: (0, 0)),
                  pl.BlockSpec((D, blk), lambda j: (0, j)),
                  pl.BlockSpec((1, blk), lambda j: (0, j))],
        out_specs=pl.BlockSpec((n, blk), lambda j: (0, j)),
        out_shape=jax.ShapeDtypeStruct((n, 3 * D), jnp.float32),
        name="adaln",
    )(c_all, w_ada, b_ada)


class _Weights:
    def __init__(self, refs):
        (self.norm_g, self.w_in, self.b_in, self.ln_v_g, self.ln_v_b, self.wm, self.bsp, self.conv_w,
         self.conv_b, self.ln_c_g, self.ln_c_b, self.w_o_a, self.w_o_b, self.w_out, self.final_g) = refs


N_WEIGHTS = 15


class _Scratch:
    def __init__(self, refs):
        (self.hb, self.zglu, self.za, self.zgm, self.vn, self.sp, self.acta, self.actb, self.actm,
         self.ya, self.yb, self.o, self.s2, self.d2) = refs


def _tile_scratch(n_streams, tlen):
    f32, bf16 = jnp.float32, jnp.bfloat16
    tm = n_streams * tlen
    return [
        pltpu.VMEM((tm, D), bf16),
        pltpu.VMEM((tm, 2 * D), f32),
        pltpu.VMEM((tm, 3 * D), f32),
        pltpu.VMEM((tm, 3 * D), f32),
        pltpu.VMEM((tm, D), bf16),
        pltpu.VMEM((tm, D), f32),
        pltpu.VMEM((tm, D), bf16),
        pltpu.VMEM((tm, D), bf16),
        pltpu.VMEM((tm, D), bf16),
        pltpu.VMEM((tm, D), f32),
        pltpu.VMEM((tm, D), f32),
        pltpu.VMEM((tm, D), f32),
        pltpu.VMEM((n_streams * (tlen + HIST_PAD) * COL_BLOCKS, LANES), f32),
        pltpu.VMEM((tm * COL_BLOCKS, LANES), f32),
    ]


def _in_proj(w, sc, dst_ref, blk0, nblk):
    cols = slice(blk0 * D, (blk0 + nblk) * D)
    z = jnp.dot(sc.hb[...], w.w_in[:, cols], preferred_element_type=jnp.float32) + w.b_in[:, cols]
    dst_ref[...] = z
    return z[0:SUBLANES, 0:LANES]


def _tile_head(n_streams, tlen, x_ref, mod_ref, w, sc):
    for s in range(n_streams):
        shift = mod_ref[s, :, 0:D]
        gs = w.norm_g[...] * (1.0 + mod_ref[s, :, D:2 * D])
        for r in range(0, tlen, ROW_CHUNK):
            x = x_ref[s, pl.ds(r, ROW_CHUNK), :]
            h = (x * _rms_scale(x)) * gs + shift
            sc.hb[pl.ds(s * tlen + r, ROW_CHUNK), :] = h.astype(jnp.bfloat16)
    return _in_proj(w, sc, sc.zglu, Z_GLU_A, 2)


def _s2_base(s, tlen):
    return s * (HIST_PAD + tlen) * COL_BLOCKS


def _tile_body(n_streams, tlen, next_head, w, sc, hist_ref, vn_out_ref, rhs_ref, conv_after=None, keep_hist=None):
    f32, bf16 = jnp.float32, jnp.bfloat16
    tm = n_streams * tlen
    chunks = [(s, r) for s in range(n_streams) for r in range(0, tlen, ROW_CHUNK)]

    for s in range(n_streams):
        rows = pl.ds(s * tlen, tlen)
        for c in range(COL_BLOCKS):
            a = sc.zglu[rows, c * LANES:(c + 1) * LANES]
            b = sc.zglu[rows, D + c * LANES:D + (c + 1) * LANES]
            sc.s2[pl.ds(_s2_base(s, tlen) + HIST_ROWS + c, tlen, stride=COL_BLOCKS), :] = a * _sigmoid_of_double(b)

    _in_proj(w, sc, sc.za, Z_U, 3)
    _in_proj(w, sc, sc.zgm, Z_GATE_B, 3)

    head_done = next_head() if next_head is not None else None

    tb = min(CONV_TBLOCK, tlen)
    token = conv_after
    for s in range(n_streams):
        base = _s2_base(s, tlen)
        for t0 in range(0, tlen, tb):
            w0 = w.conv_w[0:COL_BLOCKS, :]
            if token is not None:
                w0 = w0 + pltpu.bitcast(_zero_of(token), f32)
            acc = sc.s2[pl.ds(base + (t0 + HIST_SKIP) * COL_BLOCKS, tb * COL_BLOCKS), :] * jnp.tile(w0, (tb, 1))
            for k in range(1, CONV_WIDTH):
                wk = jnp.tile(w.conv_w[k * COL_BLOCKS:(k + 1) * COL_BLOCKS, :], (tb, 1))
                acc = acc + sc.s2[pl.ds(base + (t0 + k + HIST_SKIP) * COL_BLOCKS, tb * COL_BLOCKS), :] * wk
            sc.d2[pl.ds((s * tlen + t0) * COL_BLOCKS, tb * COL_BLOCKS), :] = acc
            token = acc[0:COL_BLOCKS, :]
            for i in range(1, tb):
                token = token + acc[i * COL_BLOCKS:(i + 1) * COL_BLOCKS, :]
        hist = sc.s2[pl.ds(base + (tlen + HIST_SKIP) * COL_BLOCKS, CONV_HIST * COL_BLOCKS), :]
        hist_ref[s] = hist if keep_hist is None else jnp.where(keep_hist, hist, hist_ref[s])
        sc.s2[pl.ds(base, HIST_ROWS), :] = sc.s2[pl.ds(base + tlen * COL_BLOCKS, HIST_ROWS), :]

    ln_v_g = w.ln_v_g[...]
    ln_v_b = w.ln_v_b[...]
    for s, r in chunks:
        rows = pl.ds(s * tlen + r, ROW_CHUNK)
        vn = _layernorm(_gelu_of_double(sc.za[rows, D:2 * D]), ln_v_g, ln_v_b)
        if vn_out_ref is not None:
            vn_out_ref[s, pl.ds(r, ROW_CHUNK), :] = vn
        sc.vn[rows, :] = vn.astype(bf16)

    if tlen % GMLP_CHUNK == 0:
        n_chunks = tm // GMLP_CHUNK
        for hd in range(N_HEADS):
            cols = slice(hd * HEAD_DIM, (hd + 1) * HEAD_DIM)
            rhs = jnp.concatenate(
                [sc.vn[n * GMLP_CHUNK:(n + 1) * GMLP_CHUNK, cols] for n in range(n_chunks)], axis=1)
            sp = jnp.dot(w.wm[hd], rhs, preferred_element_type=f32)
            for n in range(n_chunks):
                sc.sp[n * GMLP_CHUNK:(n + 1) * GMLP_CHUNK, cols] = (
                    sp[:, n * HEAD_DIM:(n + 1) * HEAD_DIM] + w.bsp[:, cols])
    else:
        rhs_ref[...] = jnp.zeros(rhs_ref.shape, bf16)
        for hd in range(N_HEADS):
            cols = slice(hd * HEAD_DIM, (hd + 1) * HEAD_DIM)
            for s in range(n_streams):
                rhs_ref[0:tlen, s * HEAD_DIM:(s + 1) * HEAD_DIM] = sc.vn[pl.ds(s * tlen, tlen), cols]
            sp = jnp.dot(w.wm[hd, 0:tlen, :], rhs_ref[...], preferred_element_type=f32)
            for s in range(n_streams):
                sc.sp[pl.ds(s * tlen, tlen), cols] = sp[:, s * HEAD_DIM:(s + 1) * HEAD_DIM] + w.bsp[0:tlen, cols]

    conv_b = w.conv_b[...]
    if head_done is not None:
        zero_row = pltpu.bitcast(_zero_of(head_done), f32)[0:1, :]
        conv_b = conv_b + jnp.tile(zero_row, (1, COL_BLOCKS))
    ln_c_g = w.ln_c_g[...]
    ln_c_b = w.ln_c_b[...]
    for s, r in chunks:
        row0 = s * tlen + r
        dw = jnp.concatenate(
            [sc.d2[pl.ds(row0 * COL_BLOCKS + c, ROW_CHUNK, stride=COL_BLOCKS), :] for c in range(COL_BLOCKS)],
            axis=1) + conv_b
        yb = _silu_of_double(_layernorm(dw, ln_c_g, ln_c_b)) * _silu_of_double(sc.zgm[pl.ds(row0, ROW_CHUNK), 0:D])
        sc.actb[pl.ds(row0, ROW_CHUNK), :] = yb.astype(bf16)
    sc.yb[...] = jnp.dot(sc.actb[...], w.w_o_b[...], preferred_element_type=f32)

    for r in range(0, tm, ROW_CHUNK):
        rows = pl.ds(r, ROW_CHUNK)
        ya = _gelu_doubled(sc.za[rows, 0:D]) * sc.sp[rows, :] * _silu_of_double(sc.za[rows, 2 * D:3 * D])
        sc.acta[rows, :] = ya.astype(bf16)
    sc.ya[...] = jnp.dot(sc.acta[...], w.w_o_a[...], preferred_element_type=f32)

    for r in range(0, tm, ROW_CHUNK):
        rows = pl.ds(r, ROW_CHUNK)
        m = ((jnp.tanh(sc.zgm[rows, D:2 * D]) + 1.0) * sc.ya[rows, :]
             + (jnp.tanh(sc.zgm[rows, 2 * D:3 * D]) + 1.0) * sc.yb[rows, :])
        sc.actm[rows, :] = m.astype(bf16)
    sc.o[...] = jnp.dot(sc.actm[...], w.w_out[...], preferred_element_type=f32)


def _tile_final(n_streams, tlen, x_ref, mod_ref, w, sc, y_ref):
    final_g = w.final_g[...]
    done = None
    for s in range(n_streams):
        gate = mod_ref[s, :, 2 * D:3 * D]
        for r in range(0, tlen, ROW_CHUNK):
            xo = x_ref[s, pl.ds(r, ROW_CHUNK), :] + gate * sc.o[pl.ds(s * tlen + r, ROW_CHUNK), :]
            y = (xo * _rms_scale(xo)) * final_g
            y_ref[s, pl.ds(r, ROW_CHUNK), :] = y
            part = y[0:SUBLANES, 0:LANES] + y[SUBLANES:ROW_CHUNK, D - LANES:D]
            done = part if done is None else done + part
    return done


def _prompt_body(tm, tiles_per_stream, n_tiles, *refs):
    xn_ref, modn_ref, xp_ref, modp_ref = refs[0:4]
    w = _Weights(refs[4:4 + N_WEIGHTS])
    y_ref, hist_ref = refs[4 + N_WEIGHTS:6 + N_WEIGHTS]
    sc = _Scratch(refs[6 + N_WEIGHTS:])
    i = pl.program_id(0)

    @pl.when(i == 0)
    def _():
        _tile_head(1, tm, xp_ref, modp_ref, w, sc)
        sc.o[...] = jnp.zeros(sc.o.shape, jnp.float32)

    @pl.when(lax.rem(i, tiles_per_stream) == 0)
    def _():
        sc.s2[pl.ds(0, HIST_ROWS), :] = jnp.zeros((HIST_ROWS, LANES), jnp.float32)

    final_done = _tile_final(1, tm, xp_ref, modp_ref, w, sc, y_ref)
    next_head = functools.partial(_tile_head, 1, tm, xn_ref, modn_ref, w, sc)
    _tile_body(1, tm, next_head, w, sc, hist_ref, None, None, conv_after=final_done, keep_hist=i < n_tiles)


def _sample_body(n_streams, tlen, *refs):
    x_ref, mod_ref, state_ref = refs[0:3]
    w = _Weights(refs[3:3 + N_WEIGHTS])
    y_ref, hist_ref, vn_out_ref, rhs_ref = refs[3 + N_WEIGHTS:7 + N_WEIGHTS]
    sc = _Scratch(refs[7 + N_WEIGHTS:])
    for s in range(n_streams):
        base = _s2_base(s, tlen)
        sc.s2[pl.ds(base, HIST_SKIP * COL_BLOCKS), :] = jnp.zeros((HIST_SKIP * COL_BLOCKS, LANES), jnp.float32)
        sc.s2[pl.ds(base + HIST_SKIP * COL_BLOCKS, CONV_HIST * COL_BLOCKS), :] = state_ref[s]
    _tile_head(n_streams, tlen, x_ref, mod_ref, w, sc)
    _tile_body(n_streams, tlen, None, w, sc, hist_ref, vn_out_ref, rhs_ref)
    _tile_final(n_streams, tlen, x_ref, mod_ref, w, sc, y_ref)


def _resident(shape, n_grid_axes):
    assert n_grid_axes == 1
    zeros = (0,) * len(shape)
    return pl.BlockSpec(shape, lambda i: zeros, pipeline_mode=pl.Buffered(1))


def _prompt_call(x, mod, weights, tm):
    bsz, tlen, _ = x.shape
    assert tlen % tm == 0 and tm % GMLP_CHUNK == 0
    nt = tlen // tm
    n_tiles = bsz * nt
    hist_shape = (1, CONV_HIST * COL_BLOCKS, LANES)

    def tile(i, shift):
        j = jnp.clip(i + shift, 0, n_tiles - 1)
        return j // nt, j % nt

    x_spec = lambda shift: pl.BlockSpec((1, tm, D), lambda i: (*tile(i, shift), 0))
    mod_spec = lambda shift: pl.BlockSpec((1, 1, 3 * D), lambda i: (tile(i, shift)[0], 0, 0))
    in_specs = [x_spec(1), mod_spec(1), x_spec(-1), mod_spec(-1)]
    in_specs += [_resident(p.shape, 1) for p in weights]
    out_specs = [x_spec(-1),
                 pl.BlockSpec(hist_shape, lambda i: (tile(i, 0)[0], 0, 0))]
    out_shape = [jax.ShapeDtypeStruct((bsz, tlen, D), jnp.float32),
                 jax.ShapeDtypeStruct((bsz,) + hist_shape[1:], jnp.float32)]
    return pl.pallas_call(
        functools.partial(_prompt_body, tm, nt, n_tiles),
        grid=(n_tiles + 1,),
        in_specs=in_specs,
        out_specs=out_specs,
        out_shape=out_shape,
        scratch_shapes=_tile_scratch(1, tm),
        compiler_params=pltpu.CompilerParams(
            dimension_semantics=("arbitrary",),
            vmem_limit_bytes=VMEM_LIMIT_BYTES),
        name="prompt_layer",
    )(x, mod, x, mod, *weights)


def _sample_call(x, mod, state2d, weights):
    bsz, tlen, _ = x.shape
    assert tlen % ROW_CHUNK == 0 and tlen < GMLP_CHUNK and bsz * HEAD_DIM == D
    hist_shape = (bsz, CONV_HIST * COL_BLOCKS, LANES)
    whole = lambda shape: pl.BlockSpec(shape, lambda i: (0,) * len(shape))
    in_specs = [whole(x.shape), whole(mod.shape), whole(hist_shape)]
    in_specs += [_resident(p.shape, 1) for p in weights]
    out_specs = [whole(x.shape), whole(hist_shape), whole(x.shape)]
    out_shape = [jax.ShapeDtypeStruct(x.shape, jnp.float32),
                 jax.ShapeDtypeStruct(hist_shape, jnp.float32),
                 jax.ShapeDtypeStruct(x.shape, jnp.float32)]
    scratch = [pltpu.VMEM((GMLP_CHUNK, bsz * HEAD_DIM), jnp.bfloat16)]
    scratch += _tile_scratch(bsz, tlen)
    return pl.pallas_call(
        functools.partial(_sample_body, bsz, tlen),
        grid=(1,),
        in_specs=in_specs,
        out_specs=out_specs,
        out_shape=out_shape,
        scratch_shapes=scratch,
        compiler_params=pltpu.CompilerParams(
            dimension_semantics=("arbitrary",),
            vmem_limit_bytes=VMEM_LIMIT_BYTES),
        name="sample_layer",
    )(x, mod, state2d, *weights)


def kernel(x_prompt, x_sample, state_conv, c_prompt, c_sample, w_ada, b_ada, norm_g, w_in, b_in, ln_v_g, ln_v_b, w_spatial, b_spatial, conv_w, conv_b, ln_c_g, ln_c_b, w_o_a, w_o_b, w_out, final_g):
    assert w_ada.shape[0] == 1, "single-layer kernel"
    bf16 = jnp.bfloat16
    n_prompt = x_prompt.shape[0]
    n_sample = x_sample.shape[0]

    mod = _adaln(jnp.concatenate([c_prompt, c_sample], axis=0), w_ada[0], b_ada[0][None])
    mod = mod[:, None, :]

    tril = jnp.tril(jnp.ones((GMLP_CHUNK, GMLP_CHUNK), dtype=bool))
    wm = jnp.where(tril[None], 0.5 * w_spatial[0], 0.0).astype(bf16)
    bsp = jnp.repeat(0.5 * b_spatial[0].T, HEAD_DIM, axis=1)
    row = lambda a: a.reshape(1, -1)
    halved = jnp.zeros((N_IN_BLOCKS,), bool).at[jnp.array(HALVED_BLOCKS)].set(True)
    in_scale = jnp.repeat(jnp.where(halved, 0.5, 1.0), D)[None, :]
    weights = (row(norm_g[0]), (w_in[0] * in_scale).astype(bf16), row(b_in[0]) * in_scale,
               row(ln_v_g[0]), row(ln_v_b[0]), wm, bsp,
               conv_w[0].reshape(CONV_WIDTH * COL_BLOCKS, LANES), row(conv_b[0]), row(0.5 * ln_c_g[0]), row(0.5 * ln_c_b[0]),
               w_o_a[0].astype(bf16), w_o_b[0].astype(bf16), (0.5 * w_out[0]).astype(bf16), row(final_g))
    assert len(weights) == N_WEIGHTS

    y_prompt, hist_prompt = _prompt_call(x_prompt, mod[:n_prompt], weights, PROMPT_TILE)
    state2d = state_conv[0].reshape(n_sample, CONV_HIST * COL_BLOCKS, LANES)
    y_sample, hist_sample, vn_sample = _sample_call(x_sample, mod[n_prompt:], state2d, weights)

    new_conv_prompt = hist_prompt.reshape(1, n_prompt, CONV_HIST, D)
    new_conv_sample = hist_sample.reshape(1, n_sample, CONV_HIST, D)
    return (y_prompt, y_sample, new_conv_prompt, new_conv_sample, vn_sample[None])
```
